```python
import jax, jax.numpy as jnp
from jax import lax
import numpy as np

D_MODEL = 2048
BATCH = 2
SEQ = 8192
DEPTH = 1
DEC_BATCH = 16
DEC_SEQ = 2048
PAST_LEN = 128

GRID_W = 64
HEAD_DIM = 128
N_Q_HEADS = 8
N_KV_HEADS = 2
Q_PER_KV = N_Q_HEADS // N_KV_HEADS
ATTN_WIDTH = N_Q_HEADS * HEAD_DIM
KV_WIDTH = N_KV_HEADS * HEAD_DIM
CONV_WIDTH = D_MODEL - ATTN_WIDTH
CONV_TAPS = 31
CONV_PAD = CONV_TAPS // 2
IN_COLS = ATTN_WIDTH + 2 * KV_WIDTH + 2 * CONV_WIDTH
D_FF = -(-(8 * D_MODEL) // (3 * 256)) * 256
Q_BLOCK = 128
ROPE_THETA = 10000.0
EPS = 1e-6

kernel_name = "hymba_conformer_gqa_axial_encoder"


def rms_norm(x, g):
    xf = x.astype(jnp.float32)
    y = xf * lax.rsqrt(jnp.mean(xf * xf, axis=-1, keepdims=True) + EPS)
    return (y * g.astype(jnp.float32)).astype(x.dtype)


def layer_norm(x, g, b):
    xf = x.astype(jnp.float32)
    mu = jnp.mean(xf, axis=-1, keepdims=True)
    xc = xf - mu
    var = jnp.mean(xc * xc, axis=-1, keepdims=True)
    y = xc * lax.rsqrt(var + EPS) * g.astype(jnp.float32) + b.astype(jnp.float32)
    return y.astype(x.dtype)


def axial_rope_tables(n):
    rows = n // GRID_W
    row_ids = jnp.repeat(jnp.arange(rows), GRID_W).astype(jnp.float32)
    col_ids = jnp.tile(jnp.arange(GRID_W), rows).astype(jnp.float32)
    half = HEAD_DIM // 2
    inv_freq = ROPE_THETA ** (-jnp.arange(0, half, 2, dtype=jnp.float32) / half)
    ang_r = row_ids[:, None] * inv_freq[None, :]
    ang_c = col_ids[:, None] * inv_freq[None, :]
    ang = jnp.concatenate([ang_r, ang_r, ang_c, ang_c], axis=-1)
    return jnp.cos(ang), jnp.sin(ang)


def apply_axial_rope(x, cos, sin):
    xf = x.astype(jnp.float32)
    xs = xf.reshape(xf.shape[:-1] + (2, 2, HEAD_DIM // 4))
    rot = jnp.stack([-xs[..., 1, :], xs[..., 0, :]], axis=-2).reshape(xf.shape)
    y = xf * cos[None, :, None, :] + rot * sin[None, :, None, :]
    return y.astype(x.dtype)


def block_attention(q, k, v):
    b, n = q.shape[0], q.shape[1]
    nblk = n // Q_BLOCK
    qb = q.reshape(b, nblk, Q_BLOCK, N_KV_HEADS, Q_PER_KV, HEAD_DIM).transpose(1, 0, 2, 3, 4, 5)
    scale = HEAD_DIM ** -0.5

    def one_block(qi):
        s = jnp.einsum('bqkgd,bskd->bkgqs', qi, k, preferred_element_type=jnp.float32) * scale
        p = jax.nn.softmax(s, axis=-1)
        return jnp.einsum('bkgqs,bskd->bqkgd', p.astype(v.dtype), v)

    out = lax.map(one_block, qb)
    return out.transpose(1, 0, 2, 3, 4, 5).reshape(b, n, ATTN_WIDTH)


def conformer_conv(u, dw_w, dw_b, ln_g, ln_b):
    a, gate = jnp.split(u, 2, axis=-1)
    h = a * jax.nn.sigmoid(gate)
    h = lax.conv_general_dilated(
        h, dw_w[:, None, :], window_strides=(1,), padding=[(CONV_PAD, CONV_PAD)],
        dimension_numbers=('NWC', 'WIO', 'NWC'), feature_group_count=CONV_WIDTH) + dw_b
    h = layer_norm(h, ln_g, ln_b)
    return jax.nn.silu(h)


def encoder_layer(x, cos, sin, norm_mix_g, w_in, b_glu, q_norm_g, k_norm_g, dw_w, dw_b,
                  conv_ln_g, conv_ln_b, w_out, norm_ffn_g, w_gate, w_up, w_down):
    b, n, _ = x.shape
    h = rms_norm(x, norm_mix_g)
    proj = jnp.einsum('bsd,dc->bsc', h, w_in)
    q = proj[..., :ATTN_WIDTH]
    k = proj[..., ATTN_WIDTH:ATTN_WIDTH + KV_WIDTH]
    v = proj[..., ATTN_WIDTH + KV_WIDTH:ATTN_WIDTH + 2 * KV_WIDTH]
    u = proj[..., ATTN_WIDTH + 2 * KV_WIDTH:] + b_glu

    q = rms_norm(q.reshape(b, n, N_Q_HEADS, HEAD_DIM), q_norm_g)
    k = rms_norm(k.reshape(b, n, N_KV_HEADS, HEAD_DIM), k_norm_g)
    q = apply_axial_rope(q, cos, sin).reshape(b, n, N_KV_HEADS, Q_PER_KV, HEAD_DIM)
    k = apply_axial_rope(k, cos, sin)
    v = v.reshape(b, n, N_KV_HEADS, HEAD_DIM)
    attn = block_attention(q, k, v)

    conv = conformer_conv(u, dw_w, dw_b, conv_ln_g, conv_ln_b)

    mix = jnp.einsum('bsc,cd->bsd', jnp.concatenate([attn, conv], axis=-1), w_out)
    x = x + mix

    h2 = rms_norm(x, norm_ffn_g)
    g = jnp.einsum('bsd,df->bsf', h2, w_gate)
    up = jnp.einsum('bsd,df->bsf', h2, w_up)
    x = x + jnp.einsum('bsf,fd->bsd', jax.nn.silu(g) * up, w_down)
    return x


def run_trunk(x, norm_mix_g, w_in, b_glu, q_norm_g, k_norm_g, dw_w, dw_b, conv_ln_g, conv_ln_b,
              w_out, norm_ffn_g, w_gate, w_up, w_down):
    cos, sin = axial_rope_tables(x.shape[1])
    for l in range(DEPTH):
        x = encoder_layer(x, cos, sin, norm_mix_g[l], w_in[l], b_glu[l], q_norm_g[l], k_norm_g[l],
                          dw_w[l], dw_b[l], conv_ln_g[l], conv_ln_b[l], w_out[l], norm_ffn_g[l],
                          w_gate[l], w_up[l], w_down[l])
    return x


def setup_inputs(seed: int = 0) -> dict:
    key = jax.random.key(seed)
    ks = jax.random.split(key, 16)
    f32 = jnp.float32

    def nrm(k, shape, scale):
        return jax.random.normal(k, shape, f32) * scale

    return {
        "x_prompt": nrm(ks[0], (BATCH, SEQ, D_MODEL), 1.0),
        "x_sample": nrm(ks[1], (DEC_BATCH, DEC_SEQ, D_MODEL), 1.0),
        "norm_mix_g": 1.0 + nrm(ks[2], (DEPTH, D_MODEL), 0.02),
        "w_in": nrm(ks[3], (DEPTH, D_MODEL, IN_COLS), D_MODEL ** -0.5),
        "b_glu": nrm(ks[4], (DEPTH, 2 * CONV_WIDTH), 0.02),
        "q_norm_g": 1.0 + nrm(ks[5], (DEPTH, HEAD_DIM), 0.02),
        "k_norm_g": 1.0 + nrm(ks[6], (DEPTH, HEAD_DIM), 0.02),
        "dw_w": nrm(ks[7], (DEPTH, CONV_TAPS, CONV_WIDTH), CONV_TAPS ** -0.5),
        "dw_b": nrm(ks[8], (DEPTH, CONV_WIDTH), 0.02),
        "conv_ln_g": 1.0 + nrm(ks[9], (DEPTH, CONV_WIDTH), 0.02),
        "conv_ln_b": nrm(ks[10], (DEPTH, CONV_WIDTH), 0.02),
        "w_out": nrm(ks[11], (DEPTH, D_MODEL, D_MODEL), D_MODEL ** -0.5),
        "norm_ffn_g": 1.0 + nrm(ks[12], (DEPTH, D_MODEL), 0.02),
        "w_gate": nrm(ks[13], (DEPTH, D_MODEL, D_FF), D_MODEL ** -0.5),
        "w_up": nrm(ks[14], (DEPTH, D_MODEL, D_FF), D_MODEL ** -0.5),
        "w_down": nrm(ks[15], (DEPTH, D_FF, D_MODEL), D_FF ** -0.5),
    }


def reference(x_prompt, x_sample, norm_mix_g, w_in, b_glu, q_norm_g, k_norm_g, dw_w, dw_b,
              conv_ln_g, conv_ln_b, w_out, norm_ffn_g, w_gate, w_up, w_down):
    y_prompt = run_trunk(x_prompt, norm_mix_g, w_in, b_glu, q_norm_g, k_norm_g, dw_w, dw_b,
                         conv_ln_g, conv_ln_b, w_out, norm_ffn_g, w_gate, w_up, w_down)
    y_sample = run_trunk(x_sample, norm_mix_g, w_in, b_glu, q_norm_g, k_norm_g, dw_w, dw_b,
                         conv_ln_g, conv_ln_b, w_out, norm_ffn_g, w_gate, w_up, w_down)
    return (y_prompt, y_sample)
```

```python
import functools

import jax
import jax.numpy as jnp
from jax import lax
from jax.experimental import pallas as pl
from jax.experimental.pallas import tpu as pltpu

D_MODEL = 2048
GRID_W = 64
HEAD_DIM = 128
N_Q_HEADS = 8
N_KV_HEADS = 2
Q_PER_KV = N_Q_HEADS // N_KV_HEADS
ATTN_WIDTH = N_Q_HEADS * HEAD_DIM
KV_WIDTH = N_KV_HEADS * HEAD_DIM
CONV_WIDTH = D_MODEL - ATTN_WIDTH
CONV_TAPS = 31
CONV_PAD = CONV_TAPS // 2
IN_COLS = ATTN_WIDTH + 2 * KV_WIDTH + 2 * CONV_WIDTH
D_FF = 5632
ROPE_THETA = 10000.0
EPS = 1e-6

F32 = jnp.float32
BF16 = jnp.bfloat16

VMEM_LIMIT_BYTES = 56 * 1024 * 1024
SUBLANES = 8
BF16_ROWS = 16
HALO_ROWS = BF16_ROWS

TM_PROJ = 512
TQ_ATTN = 256
TK_ATTN = 1024
TS_CONV = 256
ROWS_CONV = 32
TM_FFN = 512
TF_FFN = 512


def _const_spec(shape):
    return pl.BlockSpec(shape, lambda *_: (0,) * len(shape), pipeline_mode=pl.Buffered(1))


def _sigmoid(x):
    return 1.0 / (1.0 + jnp.exp(-x))


def _inproj_kernel(x_ref, g_ref, w_ref, b_ref, qg_ref, kg_ref,
                   cq_ref, sqa_ref, sqb_ref, ck_ref, ska_ref, skb_ref,
                   q_ref, k_ref, v_ref, u_ref, h_scr):
    x = x_ref[...]
    ms = jnp.mean(x * x, axis=-1, keepdims=True)
    h_scr[...] = (x * lax.rsqrt(ms + EPS) * g_ref[...]).astype(BF16)
    h = h_scr[...]

    def norm_rope(z, gain, c, sa, sb):
        z = z * lax.rsqrt(jnp.mean(z * z, axis=-1, keepdims=True) + EPS) * gain
        fwd = pltpu.roll(z, HEAD_DIM - HEAD_DIM // 4, 1)
        bwd = pltpu.roll(z, HEAD_DIM // 4, 1)
        return z * c + fwd * sa + bwd * sb

    cq, sqa, sqb = cq_ref[...], sqa_ref[...], sqb_ref[...]
    ck, ska, skb = ck_ref[...], ska_ref[...], skb_ref[...]
    qg, kg = qg_ref[...], kg_ref[...]

    half = ATTN_WIDTH // 2
    for c in range(2):
        z = jnp.dot(h, w_ref[:, c * half:(c + 1) * half], preferred_element_type=F32)
        for i in range(half // HEAD_DIM):
            zi = z[:, i * HEAD_DIM:(i + 1) * HEAD_DIM]
            q_ref[c * (half // HEAD_DIM) + i] = norm_rope(zi, qg, cq, sqa, sqb).astype(BF16)

    z = jnp.dot(h, w_ref[:, ATTN_WIDTH:ATTN_WIDTH + 2 * KV_WIDTH], preferred_element_type=F32)
    for i in range(N_KV_HEADS):
        zi = z[:, i * HEAD_DIM:(i + 1) * HEAD_DIM]
        k_ref[i] = norm_rope(zi, kg, ck, ska, skb).astype(BF16)
        v_ref[i] = z[:, KV_WIDTH + i * HEAD_DIM:KV_WIDTH + (i + 1) * HEAD_DIM].astype(BF16)

    u0 = ATTN_WIDTH + 2 * KV_WIDTH
    cw = CONV_WIDTH // 2
    for c in range(2):
        a = jnp.dot(h, w_ref[:, u0 + c * cw:u0 + (c + 1) * cw], preferred_element_type=F32)
        a = a + b_ref[:, c * cw:(c + 1) * cw]
        gt = jnp.dot(h, w_ref[:, u0 + CONV_WIDTH + c * cw:u0 + CONV_WIDTH + (c + 1) * cw],
                     preferred_element_type=F32)
        gt = gt + b_ref[:, CONV_WIDTH + c * cw:CONV_WIDTH + (c + 1) * cw]
        u_ref[:, c * cw:(c + 1) * cw] = (a * _sigmoid(gt)).astype(BF16)


def _in_projection(x2d, seq, norm_g, w_in, b_glu, q_g, k_g, tables):
    n = x2d.shape[0]
    tm = TM_PROJ
    tiles_per_seq = seq // tm
    row = lambda i: (i, 0)
    tab = lambda i: (i % tiles_per_seq, 0)
    head_major = lambda i: (0, i, 0)
    tab_spec = pl.BlockSpec((tm, HEAD_DIM), tab)
    return pl.pallas_call(
        _inproj_kernel,
        grid=(n // tm,),
        in_specs=[
            pl.BlockSpec((tm, D_MODEL), row),
            _const_spec((1, D_MODEL)),
            _const_spec((D_MODEL, IN_COLS)),
            _const_spec((1, 2 * CONV_WIDTH)),
            _const_spec((1, HEAD_DIM)),
            _const_spec((1, HEAD_DIM)),
            tab_spec, tab_spec, tab_spec, tab_spec, tab_spec, tab_spec,
        ],
        out_specs=[
            pl.BlockSpec((N_Q_HEADS, tm, HEAD_DIM), head_major),
            pl.BlockSpec((N_KV_HEADS, tm, HEAD_DIM), head_major),
            pl.BlockSpec((N_KV_HEADS, tm, HEAD_DIM), head_major),
            pl.BlockSpec((tm, CONV_WIDTH), row),
        ],
        out_shape=[
            jax.ShapeDtypeStruct((N_Q_HEADS, n, HEAD_DIM), BF16),
            jax.ShapeDtypeStruct((N_KV_HEADS, n, HEAD_DIM), BF16),
            jax.ShapeDtypeStruct((N_KV_HEADS, n, HEAD_DIM), BF16),
            jax.ShapeDtypeStruct((n, CONV_WIDTH), BF16),
        ],
        scratch_shapes=[pltpu.VMEM((tm, D_MODEL), BF16)],
        compiler_params=pltpu.CompilerParams(
            dimension_semantics=("arbitrary",), vmem_limit_bytes=VMEM_LIMIT_BYTES),
        name="in_projection",
    )(x2d, norm_g, w_in, b_glu, q_g, k_g, *tables)


def _attn_kernel(q_ref, k_ref, v_ref, o_ref, m_scr, l_scr, acc_scr, *, tq, nk):
    ki = pl.program_id(3)

    @pl.when(ki == 0)
    def _():
        m_scr[...] = jnp.full(m_scr.shape, -jnp.inf, F32)
        l_scr[...] = jnp.zeros(l_scr.shape, F32)
        acc_scr[...] = jnp.zeros(acc_scr.shape, F32)

    q = q_ref[...].reshape(Q_PER_KV * tq, HEAD_DIM)
    s = lax.dot_general(q, k_ref[0], (((1,), (1,)), ((), ())), preferred_element_type=F32)
    m_prev = m_scr[...]
    m_new = jnp.maximum(m_prev, jnp.max(s, axis=1, keepdims=True))
    alpha = jnp.exp(m_prev - m_new)
    p = jnp.exp(s - m_new)
    l_scr[...] = alpha * l_scr[...] + jnp.sum(p, axis=1, keepdims=True)
    acc_scr[...] = alpha * acc_scr[...] + jnp.dot(p.astype(BF16), v_ref[0],
                                                  preferred_element_type=F32)
    m_scr[...] = m_new

    @pl.when(ki == nk - 1)
    def _():
        o = acc_scr[...] / l_scr[...]
        for g in range(Q_PER_KV):
            o_ref[:, g * HEAD_DIM:(g + 1) * HEAD_DIM] = o[g * tq:(g + 1) * tq].astype(BF16)


def _attention(q, k, v, batch, seq):
    n = batch * seq
    tq = TQ_ATTN
    tk = min(TK_ATTN, seq)
    nq, nk = seq // tq, seq // tk
    m = Q_PER_KV * tq
    return pl.pallas_call(
        functools.partial(_attn_kernel, tq=tq, nk=nk),
        grid=(batch, N_KV_HEADS, nq, nk),
        in_specs=[
            pl.BlockSpec((Q_PER_KV, tq, HEAD_DIM), lambda b, h, qi, ki: (h, b * nq + qi, 0)),
            pl.BlockSpec((1, tk, HEAD_DIM), lambda b, h, qi, ki: (h, b * nk + ki, 0)),
            pl.BlockSpec((1, tk, HEAD_DIM), lambda b, h, qi, ki: (h, b * nk + ki, 0)),
        ],
        out_specs=pl.BlockSpec((tq, Q_PER_KV * HEAD_DIM), lambda b, h, qi, ki: (b * nq + qi, h)),
        out_shape=jax.ShapeDtypeStruct((n, ATTN_WIDTH), BF16),
        scratch_shapes=[
            pltpu.VMEM((m, 1), F32),
            pltpu.VMEM((m, 1), F32),
            pltpu.VMEM((m, HEAD_DIM), F32),
        ],
        compiler_params=pltpu.CompilerParams(
            dimension_semantics=("arbitrary",) * 4, vmem_limit_bytes=VMEM_LIMIT_BYTES),
        name="attention",
    )(q, k, v)


def _conv_kernel(prev_ref, main_ref, next_ref, w_ref, b_ref, lg_ref, lb_ref, o_ref,
                 ext_scr, sh_scr, y_scr, *, ts, tiles_per_seq):
    si = pl.program_id(0) % tiles_per_seq
    halo = HALO_ROWS
    ext_scr[0:halo, :] = jnp.where(si > 0, prev_ref[...].astype(F32), 0.0)
    ext_scr[halo:halo + ts, :] = main_ref[...].astype(F32)
    ext_scr[halo + ts:, :] = jnp.where(si < tiles_per_seq - 1, next_ref[...].astype(F32), 0.0)
    span = ts + 2 * halo - SUBLANES
    for b in range(SUBLANES):
        sh_scr[b] = ext_scr[b:b + span, :]

    base = halo - CONV_PAD
    rows = ROWS_CONV
    lanes = HEAD_DIM

    def row_block(rb, carry):
        r0 = pl.multiple_of(rb * rows, rows)
        for cb in range(CONV_WIDTH // lanes):
            cs = slice(cb * lanes, (cb + 1) * lanes)
            acc = jnp.broadcast_to(b_ref[:, cs], (rows, lanes))
            for j in range(CONV_TAPS):
                off = base + j
                win = sh_scr[off % SUBLANES, pl.ds(r0 + (off // SUBLANES) * SUBLANES, rows), cs]
                acc = acc + win * w_ref[j:j + 1, cs]
            y_scr[pl.ds(r0, rows), cs] = acc
        return carry

    lax.fori_loop(0, ts // rows, row_block, 0)

    y = y_scr[...]
    mu = jnp.mean(y, axis=-1, keepdims=True)
    yc = y - mu
    var = jnp.mean(yc * yc, axis=-1, keepdims=True)
    z = yc * lax.rsqrt(var + EPS) * lg_ref[...] + lb_ref[...]
    o_ref[...] = (z * _sigmoid(z)).astype(BF16)


def _conv_module(u, seq, dw_w, dw_b, ln_g, ln_b):
    n = u.shape[0]
    ts = TS_CONV
    tiles_per_seq = seq // ts
    hb = ts // HALO_ROWS
    last = n // HALO_ROWS - 1
    span = ts + 2 * HALO_ROWS - SUBLANES
    return pl.pallas_call(
        functools.partial(_conv_kernel, ts=ts, tiles_per_seq=tiles_per_seq),
        grid=(n // ts,),
        in_specs=[
            pl.BlockSpec((HALO_ROWS, CONV_WIDTH), lambda i: (jnp.maximum(i * hb - 1, 0), 0)),
            pl.BlockSpec((ts, CONV_WIDTH), lambda i: (i, 0)),
            pl.BlockSpec((HALO_ROWS, CONV_WIDTH), lambda i: (jnp.minimum((i + 1) * hb, last), 0)),
            _const_spec((CONV_TAPS, CONV_WIDTH)),
            _const_spec((1, CONV_WIDTH)),
            _const_spec((1, CONV_WIDTH)),
            _const_spec((1, CONV_WIDTH)),
        ],
        out_specs=pl.BlockSpec((ts, CONV_WIDTH), lambda i: (i, 0)),
        out_shape=jax.ShapeDtypeStruct((n, CONV_WIDTH), BF16),
        scratch_shapes=[
            pltpu.VMEM((ts + 2 * HALO_ROWS, CONV_WIDTH), F32),
            pltpu.VMEM((SUBLANES, span, CONV_WIDTH), F32),
            pltpu.VMEM((ts, CONV_WIDTH), F32),
        ],
        compiler_params=pltpu.CompilerParams(
            dimension_semantics=("arbitrary",), vmem_limit_bytes=VMEM_LIMIT_BYTES),
        name="conv_module",
    )(u, u, u, dw_w, dw_b, ln_g, ln_b)


def _ffn_kernel(x_ref, a_ref, c_ref, wo_ref, g_ref, wg_ref, wu_ref, wd_ref, o_ref, h_scr):
    f = pl.program_id(1)

    @pl.when(f == 0)
    def _():
        mix = jnp.dot(a_ref[...], wo_ref[0:ATTN_WIDTH, :], preferred_element_type=F32)
        mix = mix + jnp.dot(c_ref[...], wo_ref[ATTN_WIDTH:, :], preferred_element_type=F32)
        x1 = x_ref[...] + mix
        o_ref[...] = x1
        ms = jnp.mean(x1 * x1, axis=-1, keepdims=True)
        h_scr[...] = (x1 * lax.rsqrt(ms + EPS) * g_ref[...]).astype(BF16)

    h = h_scr[...]
    g = jnp.dot(h, wg_ref[...], preferred_element_type=F32)
    u = jnp.dot(h, wu_ref[...], preferred_element_type=F32)
    act = (g * _sigmoid(g) * u).astype(BF16)
    o_ref[...] += jnp.dot(act, wd_ref[...], preferred_element_type=F32)


def _out_ffn(x2d, attn, conv, w_out, norm_g, w_gate, w_up, w_down):
    n = x2d.shape[0]
    tm, tf = TM_FFN, TF_FFN
    return pl.pallas_call(
        _ffn_kernel,
        grid=(n // tm, D_FF // tf),
        in_specs=[
            pl.BlockSpec((tm, D_MODEL), lambda i, f: (i, 0)),
            pl.BlockSpec((tm, ATTN_WIDTH), lambda i, f: (i, 0)),
            pl.BlockSpec((tm, CONV_WIDTH), lambda i, f: (i, 0)),
            _const_spec((D_MODEL, D_MODEL)),
            _const_spec((1, D_MODEL)),
            pl.BlockSpec((D_MODEL, tf), lambda i, f: (0, f)),
            pl.BlockSpec((D_MODEL, tf), lambda i, f: (0, f)),
            pl.BlockSpec((tf, D_MODEL), lambda i, f: (f, 0)),
        ],
        out_specs=pl.BlockSpec((tm, D_MODEL), lambda i, f: (i, 0)),
        out_shape=jax.ShapeDtypeStruct((n, D_MODEL), F32),
        scratch_shapes=[pltpu.VMEM((tm, D_MODEL), BF16)],
        compiler_params=pltpu.CompilerParams(
            dimension_semantics=("arbitrary", "arbitrary"), vmem_limit_bytes=VMEM_LIMIT_BYTES),
        name="out_ffn",
    )(x2d, attn, conv, w_out, norm_g, w_gate, w_up, w_down)


def _rope_tables(seq):
    rows = seq // GRID_W
    row_ids = jnp.repeat(jnp.arange(rows), GRID_W).astype(F32)
    col_ids = jnp.tile(jnp.arange(GRID_W), rows).astype(F32)
    half = HEAD_DIM // 2
    inv_freq = ROPE_THETA ** (-jnp.arange(0, half, 2, dtype=F32) / half)
    ang_r = row_ids[:, None] * inv_freq[None, :]
    ang_c = col_ids[:, None] * inv_freq[None, :]
    ang = jnp.concatenate([ang_r, ang_r, ang_c, ang_c], axis=-1)
    cos, sin = jnp.cos(ang), jnp.sin(ang)
    first = (jnp.arange(HEAD_DIM) // (HEAD_DIM // 4)) % 2 == 0
    sin_a = jnp.where(first[None, :], -sin, 0.0)
    sin_b = jnp.where(first[None, :], 0.0, sin)
    scale = HEAD_DIM ** -0.5
    return (cos * scale, sin_a * scale, sin_b * scale, cos, sin_a, sin_b)


def _trunk(x, p):
    batch, seq, _ = x.shape
    x2d = x.reshape(batch * seq, D_MODEL)
    q, k, v, u = _in_projection(x2d, seq, p["norm_mix_g"], p["w_in"], p["b_glu"],
                                p["q_norm_g"], p["k_norm_g"], _rope_tables(seq))
    attn = _attention(q, k, v, batch, seq)
    conv = _conv_module(u, seq, p["dw_w"], p["dw_b"], p["conv_ln_g"], p["conv_ln_b"])
    y = _out_ffn(x2d, attn, conv, p["w_out"], p["norm_ffn_g"], p["w_gate"], p["w_up"], p["w_down"])
    return y.reshape(batch, seq, D_MODEL)


def _prepare(norm_mix_g, w_in, b_glu, q_norm_g, k_norm_g, dw_w, dw_b,
             conv_ln_g, conv_ln_b, w_out, norm_ffn_g, w_gate, w_up, w_down):
    assert norm_mix_g.shape[0] == 1, "single-layer trunk"
    return {
        "norm_mix_g": norm_mix_g[0][None, :],
        "w_in": w_in[0].astype(BF16),
        "b_glu": b_glu[0][None, :],
        "q_norm_g": q_norm_g[0][None, :],
        "k_norm_g": k_norm_g[0][None, :],
        "dw_w": dw_w[0],
        "dw_b": dw_b[0][None, :],
        "conv_ln_g": conv_ln_g[0][None, :],
        "conv_ln_b": conv_ln_b[0][None, :],
        "w_out": w_out[0].astype(BF16),
        "norm_ffn_g": norm_ffn_g[0][None, :],
        "w_gate": w_gate[0].astype(BF16),
        "w_up": w_up[0].astype(BF16),
        "w_down": w_down[0].astype(BF16),
    }


def kernel(x_prompt, x_sample, norm_mix_g, w_in, b_glu, q_norm_g, k_norm_g, dw_w, dw_b,
           conv_ln_g, conv_ln_b, w_out, norm_ffn_g, w_gate, w_up, w_down):
    p = _prepare(norm_mix_g, w_in, b_glu, q_norm_g, k_norm_g, dw_w, dw_b,
                 conv_ln_g, conv_ln_b, w_out, norm_ffn_g, w_gate, w_up, w_down)
    return (_trunk(x_prompt, p), _trunk(x_sample, p))
```

```python
import functools

import jax
import jax.numpy as jnp
from jax import lax
from jax.experimental import pallas as pl
from jax.experimental.pallas import tpu as pltpu

D_MODEL = 2048
GRID_W = 64
HEAD_DIM = 128
N_Q_HEADS = 8
N_KV_HEADS = 2
Q_PER_KV = N_Q_HEADS // N_KV_HEADS
ATTN_WIDTH = N_Q_HEADS * HEAD_DIM
KV_WIDTH = N_KV_HEADS * HEAD_DIM
CONV_WIDTH = D_MODEL - ATTN_WIDTH
CONV_TAPS = 31
CONV_PAD = CONV_TAPS // 2
IN_COLS = ATTN_WIDTH + 2 * KV_WIDTH + 2 * CONV_WIDTH
D_FF = 5632
ROPE_THETA = 10000.0
EPS = 1e-6
LOG2_E = 1.4426950408889634

F32 = jnp.float32
BF16 = jnp.bfloat16

VMEM_LIMIT_BYTES = 56 * 1024 * 1024
SUBLANES = 8
BF16_ROWS = 16
HALO_ROWS = BF16_ROWS

TM_PROJ = 512
TQ_ATTN = 256
TK_ATTN = 1024
QS_ATTN = 2048
TS_CONV = 256
ROWS_CONV = 32
TM_FFN = 512
TF_FFN = 512


def _const_spec(shape):
    return pl.BlockSpec(shape, lambda *_: (0,) * len(shape), pipeline_mode=pl.Buffered(1))


def _sigmoid(x):
    return 1.0 / (1.0 + jnp.exp(-x))


def _inproj_kernel(x_ref, g_ref, w_ref, b_ref, qg_ref, kg_ref,
                   cq_ref, sqa_ref, sqb_ref, ck_ref, ska_ref, skb_ref,
                   q_ref, k_ref, v_ref, u_ref, h_scr):
    x = x_ref[...]
    ms = jnp.mean(x * x, axis=-1, keepdims=True)
    h_scr[...] = (x * lax.rsqrt(ms + EPS) * g_ref[...]).astype(BF16)
    h = h_scr[...]

    def norm_rope(z, gain, c, sa, sb):
        z = z * lax.rsqrt(jnp.mean(z * z, axis=-1, keepdims=True) + EPS) * gain
        fwd = pltpu.roll(z, HEAD_DIM - HEAD_DIM // 4, 1)
        bwd = pltpu.roll(z, HEAD_DIM // 4, 1)
        return z * c + fwd * sa + bwd * sb

    cq, sqa, sqb = cq_ref[...], sqa_ref[...], sqb_ref[...]
    ck, ska, skb = ck_ref[...], ska_ref[...], skb_ref[...]
    qg, kg = qg_ref[...], kg_ref[...]

    half = ATTN_WIDTH // 2
    for c in range(2):
        z = jnp.dot(h, w_ref[:, c * half:(c + 1) * half], preferred_element_type=F32)
        for i in range(half // HEAD_DIM):
            zi = z[:, i * HEAD_DIM:(i + 1) * HEAD_DIM]
            q_ref[c * (half // HEAD_DIM) + i] = norm_rope(zi, qg, cq, sqa, sqb).astype(BF16)

    z = jnp.dot(h, w_ref[:, ATTN_WIDTH:ATTN_WIDTH + 2 * KV_WIDTH], preferred_element_type=F32)
    for i in range(N_KV_HEADS):
        zi = z[:, i * HEAD_DIM:(i + 1) * HEAD_DIM]
        k_ref[i] = norm_rope(zi, kg, ck, ska, skb).astype(BF16)
        v_ref[i, :, 0:HEAD_DIM] = z[:, KV_WIDTH + i * HEAD_DIM:KV_WIDTH + (i + 1) * HEAD_DIM].astype(BF16)
        v_ref[i, :, HEAD_DIM:] = jnp.ones((x.shape[0], HEAD_DIM), BF16)

    u0 = ATTN_WIDTH + 2 * KV_WIDTH
    cw = CONV_WIDTH // 2
    for c in range(2):
        a = jnp.dot(h, w_ref[:, u0 + c * cw:u0 + (c + 1) * cw], preferred_element_type=F32)
        a = a + b_ref[:, c * cw:(c + 1) * cw]
        gt = jnp.dot(h, w_ref[:, u0 + CONV_WIDTH + c * cw:u0 + CONV_WIDTH + (c + 1) * cw],
                     preferred_element_type=F32)
        gt = gt + b_ref[:, CONV_WIDTH + c * cw:CONV_WIDTH + (c + 1) * cw]
        u_ref[:, c * cw:(c + 1) * cw] = (a * _sigmoid(gt)).astype(BF16)


def _in_projection(x2d, seq, norm_g, w_in, b_glu, q_g, k_g, tables):
    n = x2d.shape[0]
    tm = TM_PROJ
    tiles_per_seq = seq // tm
    row = lambda i: (i, 0)
    tab = lambda i: (i % tiles_per_seq, 0)
    head_major = lambda i: (0, i, 0)
    tab_spec = pl.BlockSpec((tm, HEAD_DIM), tab)
    return pl.pallas_call(
        _inproj_kernel,
        grid=(n // tm,),
        in_specs=[
            pl.BlockSpec((tm, D_MODEL), row),
            _const_spec((1, D_MODEL)),
            _const_spec((D_MODEL, IN_COLS)),
            _const_spec((1, 2 * CONV_WIDTH)),
            _const_spec((1, HEAD_DIM)),
            _const_spec((1, HEAD_DIM)),
            tab_spec, tab_spec, tab_spec, tab_spec, tab_spec, tab_spec,
        ],
        out_specs=[
            pl.BlockSpec((N_Q_HEADS, tm, HEAD_DIM), head_major),
            pl.BlockSpec((N_KV_HEADS, tm, HEAD_DIM), head_major),
            pl.BlockSpec((N_KV_HEADS, tm, 2 * HEAD_DIM), head_major),
            pl.BlockSpec((tm, CONV_WIDTH), row),
        ],
        out_shape=[
            jax.ShapeDtypeStruct((N_Q_HEADS, n, HEAD_DIM), BF16),
            jax.ShapeDtypeStruct((N_KV_HEADS, n, HEAD_DIM), BF16),
            jax.ShapeDtypeStruct((N_KV_HEADS, n, 2 * HEAD_DIM), BF16),
            jax.ShapeDtypeStruct((n, CONV_WIDTH), BF16),
        ],
        scratch_shapes=[pltpu.VMEM((tm, D_MODEL), BF16)],
        compiler_params=pltpu.CompilerParams(
            dimension_semantics=("arbitrary",), vmem_limit_bytes=VMEM_LIMIT_BYTES),
        name="in_projection",
    )(x2d, norm_g, w_in, b_glu, q_g, k_g, *tables)


def _attn_kernel(q_ref, k_ref, v_ref, o_ref, s0, s1, p0, p1, a0, a1, m_scr, acc_scr,
                 *, tq, tk, nq, nk):
    m_rows = Q_PER_KV * tq
    n_pairs = nq * nk
    s_buf, p_buf, a_buf = (s0, s1), (p0, p1), (a0, a1)

    def split(t):
        t = jnp.minimum(t, n_pairs - 1)
        return t // nk, t % nk

    def stage_a(t, par):
        qi, kj = split(t)
        q = q_ref[:, pl.ds(pl.multiple_of(qi * tq, tq), tq), :].reshape(m_rows, HEAD_DIM)
        k = k_ref[0, pl.ds(pl.multiple_of(kj * tk, tk), tk), :]
        s_buf[par][...] = lax.dot_general(q, k, (((1,), (1,)), ((), ())),
                                          preferred_element_type=F32)

    def stage_b(t, par):
        _, kj = split(t)
        tile_max = jnp.max(s_buf[par][...], axis=1, keepdims=True)
        m_prev = jnp.where(kj == 0, -jnp.inf, m_scr[...])
        m_new = jnp.maximum(m_prev, tile_max)
        a_buf[par][...] = jnp.exp2(m_prev - m_new)
        m_scr[...] = m_new
        p_buf[par][...] = jnp.exp2(s_buf[par][...] - m_new).astype(BF16)

    def stage_c(t, par):
        qi, kj = split(t)
        v = v_ref[0, pl.ds(pl.multiple_of(kj * tk, tk), tk), :]
        rows = pl.ds(pl.multiple_of(qi * tq, tq), tq)
        for g in range(Q_PER_KV):
            head = slice(g * tq, (g + 1) * tq)
            acc = acc_scr[head, :] * a_buf[par][head, :]
            acc = acc + jnp.dot(p_buf[par][head, :], v, preferred_element_type=F32)
            acc_scr[head, :] = acc
            o = acc[:, 0:HEAD_DIM] / acc[:, HEAD_DIM:HEAD_DIM + 1]
            o_ref[rows, g * HEAD_DIM:(g + 1) * HEAD_DIM] = o.astype(BF16)

    m_scr[...] = jnp.zeros(m_scr.shape, F32)
    acc_scr[...] = jnp.zeros(acc_scr.shape, F32)
    stage_a(0, 0)
    stage_a(1, 1)
    stage_b(0, 0)

    def body(i, carry):
        for par in range(2):
            t = 2 * i + par
            stage_a(t + 2, par)
            stage_b(t + 1, 1 - par)
            stage_c(t, par)
        return carry

    lax.fori_loop(0, n_pairs // 2, body, 0)


def _attention(q, k, v, batch, seq):
    n = batch * seq
    tq = TQ_ATTN
    tk = min(TK_ATTN, seq)
    qs = min(QS_ATTN, seq)
    nq, nk, nsuper = qs // tq, seq // tk, seq // qs
    assert (nq * nk) % 2 == 0
    m = Q_PER_KV * tq
    kv_map = lambda b, h, si: (h, b, 0)
    col = lambda: pltpu.VMEM((m, 1), F32)
    return pl.pallas_call(
        functools.partial(_attn_kernel, tq=tq, tk=tk, nq=nq, nk=nk),
        grid=(batch, N_KV_HEADS, nsuper),
        in_specs=[
            pl.BlockSpec((Q_PER_KV, qs, HEAD_DIM), lambda b, h, si: (h, b * nsuper + si, 0)),
            pl.BlockSpec((1, seq, HEAD_DIM), kv_map),
            pl.BlockSpec((1, seq, 2 * HEAD_DIM), kv_map),
        ],
        out_specs=pl.BlockSpec((qs, Q_PER_KV * HEAD_DIM), lambda b, h, si: (b * nsuper + si, h)),
        out_shape=jax.ShapeDtypeStruct((n, ATTN_WIDTH), BF16),
        scratch_shapes=[
            pltpu.VMEM((m, tk), F32), pltpu.VMEM((m, tk), F32),
            pltpu.VMEM((m, tk), BF16), pltpu.VMEM((m, tk), BF16),
            col(), col(), col(),
            pltpu.VMEM((m, 2 * HEAD_DIM), F32),
        ],
        compiler_params=pltpu.CompilerParams(
            dimension_semantics=("arbitrary",) * 3, vmem_limit_bytes=VMEM_LIMIT_BYTES),
        name="attention",
    )(q, k, v)


def _conv_kernel(prev_ref, main_ref, next_ref, w_ref, b_ref, lg_ref, lb_ref, o_ref,
                 ext_scr, sh_scr, y_scr, *, ts, tiles_per_seq):
    si = pl.program_id(0) % tiles_per_seq
    halo = HALO_ROWS
    ext_scr[0:halo, :] = jnp.where(si > 0, prev_ref[...].astype(F32), 0.0)
    ext_scr[halo:halo + ts, :] = main_ref[...].astype(F32)
    ext_scr[halo + ts:, :] = jnp.where(si < tiles_per_seq - 1, next_ref[...].astype(F32), 0.0)
    span = ts + 2 * halo - SUBLANES
    for b in range(SUBLANES):
        sh_scr[b] = ext_scr[b:b + span, :]

    base = halo - CONV_PAD
    rows = ROWS_CONV
    lanes = HEAD_DIM

    def row_block(rb, carry):
        r0 = pl.multiple_of(rb * rows, rows)
        for cb in range(CONV_WIDTH // lanes):
            cs = slice(cb * lanes, (cb + 1) * lanes)
            acc = jnp.broadcast_to(b_ref[:, cs], (rows, lanes))
            for j in range(CONV_TAPS):
                off = base + j
                win = sh_scr[off % SUBLANES, pl.ds(r0 + (off // SUBLANES) * SUBLANES, rows), cs]
                acc = acc + win * w_ref[j:j + 1, cs]
            y_scr[pl.ds(r0, rows), cs] = acc
        return carry

    lax.fori_loop(0, ts // rows, row_block, 0)

    y = y_scr[...]
    mu = jnp.mean(y, axis=-1, keepdims=True)
    yc = y - mu
    var = jnp.mean(yc * yc, axis=-1, keepdims=True)
    z = yc * lax.rsqrt(var + EPS) * lg_ref[...] + lb_ref[...]
    o_ref[...] = (z * _sigmoid(z)).astype(BF16)


def _conv_module(u, seq, dw_w, dw_b, ln_g, ln_b):
    n = u.shape[0]
    ts = TS_CONV
    tiles_per_seq = seq // ts
    hb = ts // HALO_ROWS
    last = n // HALO_ROWS - 1
    span = ts + 2 * HALO_ROWS - SUBLANES
    return pl.pallas_call(
        functools.partial(_conv_kernel, ts=ts, tiles_per_seq=tiles_per_seq),
        grid=(n // ts,),
        in_specs=[
            pl.BlockSpec((HALO_ROWS, CONV_WIDTH), lambda i: (jnp.maximum(i * hb - 1, 0), 0)),
            pl.BlockSpec((ts, CONV_WIDTH), lambda i: (i, 0)),
            pl.BlockSpec((HALO_ROWS, CONV_WIDTH), lambda i: (jnp.minimum((i + 1) * hb, last), 0)),
            _const_spec((CONV_TAPS, CONV_WIDTH)),
            _const_spec((1, CONV_WIDTH)),
            _const_spec((1, CONV_WIDTH)),
            _const_spec((1, CONV_WIDTH)),
        ],
        out_specs=pl.BlockSpec((ts, CONV_WIDTH), lambda i: (i, 0)),
        out_shape=jax.ShapeDtypeStruct((n, CONV_WIDTH), BF16),
        scratch_shapes=[
            pltpu.VMEM((ts + 2 * HALO_ROWS, CONV_WIDTH), F32),
            pltpu.VMEM((SUBLANES, span, CONV_WIDTH), F32),
            pltpu.VMEM((ts, CONV_WIDTH), F32),
        ],
        compiler_params=pltpu.CompilerParams(
            dimension_semantics=("arbitrary",), vmem_limit_bytes=VMEM_LIMIT_BYTES),
        name="conv_module",
    )(u, u, u, dw_w, dw_b, ln_g, ln_b)


def _ffn_kernel(x_ref, a_ref, c_ref, wo_ref, g_ref, wg_ref, wu_ref, wd_ref, o_ref, h_scr):
    f = pl.program_id(1)

    @pl.when(f == 0)
    def _():
        mix = jnp.dot(a_ref[...], wo_ref[0:ATTN_WIDTH, :], preferred_element_type=F32)
        mix = mix + jnp.dot(c_ref[...], wo_ref[ATTN_WIDTH:, :], preferred_element_type=F32)
        x1 = x_ref[...] + mix
        o_ref[...] = x1
        ms = jnp.mean(x1 * x1, axis=-1, keepdims=True)
        h_scr[...] = (x1 * lax.rsqrt(ms + EPS) * g_ref[...]).astype(BF16)

    h = h_scr[...]
    g = jnp.dot(h, wg_ref[...], preferred_element_type=F32)
    u = jnp.dot(h, wu_ref[...], preferred_element_type=F32)
    act = (g * _sigmoid(g) * u).astype(BF16)
    o_ref[...] += jnp.dot(act, wd_ref[...], preferred_element_type=F32)


def _out_ffn(x2d, attn, conv, w_out, norm_g, w_gate, w_up, w_down):
    n = x2d.shape[0]
    tm, tf = TM_FFN, TF_FFN
    return pl.pallas_call(
        _ffn_kernel,
        grid=(n // tm, D_FF // tf),
        in_specs=[
            pl.BlockSpec((tm, D_MODEL), lambda i, f: (i, 0)),
            pl.BlockSpec((tm, ATTN_WIDTH), lambda i, f: (i, 0)),
            pl.BlockSpec((tm, CONV_WIDTH), lambda i, f: (i, 0)),
            _const_spec((D_MODEL, D_MODEL)),
            _const_spec((1, D_MODEL)),
            pl.BlockSpec((D_MODEL, tf), lambda i, f: (0, f)),
            pl.BlockSpec((D_MODEL, tf), lambda i, f: (0, f)),
            pl.BlockSpec((tf, D_MODEL), lambda i, f: (f, 0)),
        ],
        out_specs=pl.BlockSpec((tm, D_MODEL), lambda i, f: (i, 0)),
        out_shape=jax.ShapeDtypeStruct((n, D_MODEL), F32),
        scratch_shapes=[pltpu.VMEM((tm, D_MODEL), BF16)],
        compiler_params=pltpu.CompilerParams(
            dimension_semantics=("arbitrary", "arbitrary"), vmem_limit_bytes=VMEM_LIMIT_BYTES),
        name="out_ffn",
    )(x2d, attn, conv, w_out, norm_g, w_gate, w_up, w_down)


def _rope_tables(seq):
    rows = seq // GRID_W
    row_ids = jnp.repeat(jnp.arange(rows), GRID_W).astype(F32)
    col_ids = jnp.tile(jnp.arange(GRID_W), rows).astype(F32)
    half = HEAD_DIM // 2
    inv_freq = ROPE_THETA ** (-jnp.arange(0, half, 2, dtype=F32) / half)
    ang_r = row_ids[:, None] * inv_freq[None, :]
    ang_c = col_ids[:, None] * inv_freq[None, :]
    ang = jnp.concatenate([ang_r, ang_r, ang_c, ang_c], axis=-1)
    cos, sin = jnp.cos(ang), jnp.sin(ang)
    first = (jnp.arange(HEAD_DIM) // (HEAD_DIM // 4)) % 2 == 0
    sin_a = jnp.where(first[None, :], -sin, 0.0)
    sin_b = jnp.where(first[None, :], 0.0, sin)
    scale = HEAD_DIM ** -0.5 * LOG2_E
    return (cos * scale, sin_a * scale, sin_b * scale, cos, sin_a, sin_b)


def _trunk(x, p):
    batch, seq, _ = x.shape
    x2d = x.reshape(batch * seq, D_MODEL)
    q, k, v, u = _in_projection(x2d, seq, p["norm_mix_g"], p["w_in"], p["b_glu"],
                                p["q_norm_g"], p["k_norm_g"], _rope_tables(seq))
    attn = _attention(q, k, v, batch, seq)
    conv = _conv_module(u, seq, p["dw_w"], p["dw_b"], p["conv_ln_g"], p["conv_ln_b"])
    y = _out_ffn(x2d, attn, conv, p["w_out"], p["norm_ffn_g"], p["w_gate"], p["w_up"], p["w_down"])
    return y.reshape(batch, seq, D_MODEL)


def _prepare(norm_mix_g, w_in, b_glu, q_norm_g, k_norm_g, dw_w, dw_b,
             conv_ln_g, conv_ln_b, w_out, norm_ffn_g, w_gate, w_up, w_down):
    assert norm_mix_g.shape[0] == 1, "single-layer trunk"
    return {
        "norm_mix_g": norm_mix_g[0][None, :],
        "w_in": w_in[0].astype(BF16),
        "b_glu": b_glu[0][None, :],
        "q_norm_g": q_norm_g[0][None, :],
        "k_norm_g": k_norm_g[0][None, :],
        "dw_w": dw_w[0],
        "dw_b": dw_b[0][None, :],
        "conv_ln_g": conv_ln_g[0][None, :],
        "conv_ln_b": conv_ln_b[0][None, :],
        "w_out": w_out[0].astype(BF16),
        "norm_ffn_g": norm_ffn_g[0][None, :],
        "w_gate": w_gate[0].astype(BF16),
        "w_up": w_up[0].astype(BF16),
        "w_down": w_down[0].astype(BF16),
    }


def kernel(x_prompt, x_sample, norm_mix_g, w_in, b_glu, q_norm_g, k_norm_g, dw_w, dw_b,
           conv_ln_g, conv_ln_b, w_out, norm_ffn_g, w_gate, w_up, w_down):
    p = _prepare(norm_mix_g, w_in, b_glu, q_norm_g, k_norm_g, dw_w, dw_b,
                 conv_ln_g, conv_ln_b, w_out, norm_ffn_g, w_gate, w_up, w_down)
    return (_trunk(x_prompt, p), _trunk(x_sample, p))
```

```python
import functools

import jax
import jax.numpy as jnp
from jax import lax
from jax.experimental import pallas as pl
from jax.experimental.pallas import tpu as pltpu

D_MODEL = 2048
GRID_W = 64
HEAD_DIM = 128
N_Q_HEADS = 8
N_KV_HEADS = 2
Q_PER_KV = N_Q_HEADS // N_KV_HEADS
ATTN_WIDTH = N_Q_HEADS * HEAD_DIM
KV_WIDTH = N_KV_HEADS * HEAD_DIM
CONV_WIDTH = D_MODEL - ATTN_WIDTH
CONV_TAPS = 31
CONV_PAD = CONV_TAPS // 2
IN_COLS = ATTN_WIDTH + 2 * KV_WIDTH + 2 * CONV_WIDTH
D_FF = 5632
ROPE_THETA = 10000.0
EPS = 1e-6
LOG2_E = 1.4426950408889634

F32 = jnp.float32
BF16 = jnp.bfloat16

VMEM_LIMIT_BYTES = 56 * 1024 * 1024
SUBLANES = 8
BF16_ROWS = 16
HALO_ROWS = BF16_ROWS

TM_PROJ = 512
TQ_ATTN = 256
TK_ATTN = 1024
QS_ATTN = 2048
TS_CONV = 256
ROWS_CONV = 64
TM_FFN = 512
TF_FFN = 512


def _const_spec(shape):
    return pl.BlockSpec(shape, lambda *_: (0,) * len(shape), pipeline_mode=pl.Buffered(1))


def _sigmoid(x):
    return 1.0 / (1.0 + jnp.exp(-x))


def _inproj_kernel(x_ref, g_ref, w_ref, b_ref, qg_ref, kg_ref,
                   cq_ref, sqa_ref, sqb_ref, ck_ref, ska_ref, skb_ref,
                   q_ref, k_ref, v_ref, u_ref, h_scr):
    x = x_ref[...]
    ms = jnp.mean(x * x, axis=-1, keepdims=True)
    h_scr[...] = (x * lax.rsqrt(ms + EPS) * g_ref[...]).astype(BF16)
    h = h_scr[...]

    def norm_rope(z, gain, c, sa, sb):
        z = z * lax.rsqrt(jnp.mean(z * z, axis=-1, keepdims=True) + EPS) * gain
        fwd = pltpu.roll(z, HEAD_DIM - HEAD_DIM // 4, 1)
        bwd = pltpu.roll(z, HEAD_DIM // 4, 1)
        return z * c + fwd * sa + bwd * sb

    cq, sqa, sqb = cq_ref[...], sqa_ref[...], sqb_ref[...]
    ck, ska, skb = ck_ref[...], ska_ref[...], skb_ref[...]
    qg, kg = qg_ref[...], kg_ref[...]

    half = ATTN_WIDTH // 2
    for c in range(2):
        z = jnp.dot(h, w_ref[:, c * half:(c + 1) * half], preferred_element_type=F32)
        for i in range(half // HEAD_DIM):
            zi = z[:, i * HEAD_DIM:(i + 1) * HEAD_DIM]
            q_ref[c * (half // HEAD_DIM) + i] = norm_rope(zi, qg, cq, sqa, sqb).astype(BF16)

    z = jnp.dot(h, w_ref[:, ATTN_WIDTH:ATTN_WIDTH + 2 * KV_WIDTH], preferred_element_type=F32)
    for i in range(N_KV_HEADS):
        zi = z[:, i * HEAD_DIM:(i + 1) * HEAD_DIM]
        k_ref[i] = norm_rope(zi, kg, ck, ska, skb).astype(BF16)
        v_ref[i, :, 0:HEAD_DIM] = z[:, KV_WIDTH + i * HEAD_DIM:KV_WIDTH + (i + 1) * HEAD_DIM].astype(BF16)
        v_ref[i, :, HEAD_DIM:] = jnp.ones((x.shape[0], HEAD_DIM), BF16)

    u0 = ATTN_WIDTH + 2 * KV_WIDTH
    cw = CONV_WIDTH // 2
    for c in range(2):
        a = jnp.dot(h, w_ref[:, u0 + c * cw:u0 + (c + 1) * cw], preferred_element_type=F32)
        a = a + b_ref[:, c * cw:(c + 1) * cw]
        gt = jnp.dot(h, w_ref[:, u0 + CONV_WIDTH + c * cw:u0 + CONV_WIDTH + (c + 1) * cw],
                     preferred_element_type=F32)
        gt = gt + b_ref[:, CONV_WIDTH + c * cw:CONV_WIDTH + (c + 1) * cw]
        u_ref[:, c * cw:(c + 1) * cw] = (a * _sigmoid(gt)).astype(BF16)


def _in_projection(x2d, seq, norm_g, w_in, b_glu, q_g, k_g, tables):
    n = x2d.shape[0]
    tm = TM_PROJ
    tiles_per_seq = seq // tm
    row = lambda i: (i, 0)
    tab = lambda i: (i % tiles_per_seq, 0)
    head_major = lambda i: (0, i, 0)
    tab_spec = pl.BlockSpec((tm, HEAD_DIM), tab)
    return pl.pallas_call(
        _inproj_kernel,
        grid=(n // tm,),
        in_specs=[
            pl.BlockSpec((tm, D_MODEL), row),
            _const_spec((1, D_MODEL)),
            _const_spec((D_MODEL, IN_COLS)),
            _const_spec((1, 2 * CONV_WIDTH)),
            _const_spec((1, HEAD_DIM)),
            _const_spec((1, HEAD_DIM)),
            tab_spec, tab_spec, tab_spec, tab_spec, tab_spec, tab_spec,
        ],
        out_specs=[
            pl.BlockSpec((N_Q_HEADS, tm, HEAD_DIM), head_major),
            pl.BlockSpec((N_KV_HEADS, tm, HEAD_DIM), head_major),
            pl.BlockSpec((N_KV_HEADS, tm, 2 * HEAD_DIM), head_major),
            pl.BlockSpec((tm, CONV_WIDTH), row),
        ],
        out_shape=[
            jax.ShapeDtypeStruct((N_Q_HEADS, n, HEAD_DIM), BF16),
            jax.ShapeDtypeStruct((N_KV_HEADS, n, HEAD_DIM), BF16),
            jax.ShapeDtypeStruct((N_KV_HEADS, n, 2 * HEAD_DIM), BF16),
            jax.ShapeDtypeStruct((n, CONV_WIDTH), BF16),
        ],
        scratch_shapes=[pltpu.VMEM((tm, D_MODEL), BF16)],
        compiler_params=pltpu.CompilerParams(
            dimension_semantics=("arbitrary",), vmem_limit_bytes=VMEM_LIMIT_BYTES),
        name="in_projection",
    )(x2d, norm_g, w_in, b_glu, q_g, k_g, *tables)


def _attn_kernel(q_ref, k_ref, v_ref, o_ref, s0, s1, m_scr, acc_scr, *, tq, tk, nq, nk):
    m_rows = Q_PER_KV * tq
    n_pairs = nq * nk
    s_buf = (s0, s1)

    def split(t):
        t = jnp.minimum(t, n_pairs - 1)
        return t // nk, t % nk

    def stage_a(t, par):
        qi, kj = split(t)
        q = q_ref[:, pl.ds(pl.multiple_of(qi * tq, tq), tq), :].reshape(m_rows, HEAD_DIM)
        k = k_ref[0, pl.ds(pl.multiple_of(kj * tk, tk), tk), :]
        s_buf[par][...] = lax.dot_general(q, k, (((1,), (1,)), ((), ())),
                                          preferred_element_type=F32)

    def stage_b(t, par):
        qi, kj = split(t)
        v = v_ref[0, pl.ds(pl.multiple_of(kj * tk, tk), tk), :]
        rows = pl.ds(pl.multiple_of(qi * tq, tq), tq)
        for g in range(Q_PER_KV):
            head = slice(g * tq, (g + 1) * tq)
            tile_max = jnp.max(s_buf[par][head, :], axis=1, keepdims=True)
            m_prev = jnp.where(kj == 0, -jnp.inf, m_scr[head, :])
            m_new = jnp.maximum(m_prev, tile_max)
            alpha = jnp.exp2(m_prev - m_new)
            m_scr[head, :] = m_new
            p = jnp.exp2(s_buf[par][head, :] - m_new).astype(BF16)
            acc = acc_scr[head, :] * alpha
            acc = acc + jnp.dot(p, v, preferred_element_type=F32)
            acc_scr[head, :] = acc
            o = acc[:, 0:HEAD_DIM] / acc[:, HEAD_DIM:HEAD_DIM + 1]
            o_ref[rows, g * HEAD_DIM:(g + 1) * HEAD_DIM] = o.astype(BF16)

    m_scr[...] = jnp.zeros(m_scr.shape, F32)
    acc_scr[...] = jnp.zeros(acc_scr.shape, F32)
    stage_a(0, 0)

    def body(i, carry):
        for par in range(2):
            t = 2 * i + par
            stage_a(t + 1, 1 - par)
            stage_b(t, par)
        return carry

    lax.fori_loop(0, n_pairs // 2, body, 0)


def _attention(q, k, v, batch, seq):
    n = batch * seq
    tq = TQ_ATTN
    tk = min(TK_ATTN, seq)
    qs = min(QS_ATTN, seq)
    nq, nk, nsuper = qs // tq, seq // tk, seq // qs
    assert (nq * nk) % 2 == 0
    m = Q_PER_KV * tq
    kv_map = lambda b, h, si: (h, b, 0)
    return pl.pallas_call(
        functools.partial(_attn_kernel, tq=tq, tk=tk, nq=nq, nk=nk),
        grid=(batch, N_KV_HEADS, nsuper),
        in_specs=[
            pl.BlockSpec((Q_PER_KV, qs, HEAD_DIM), lambda b, h, si: (h, b * nsuper + si, 0)),
            pl.BlockSpec((1, seq, HEAD_DIM), kv_map),
            pl.BlockSpec((1, seq, 2 * HEAD_DIM), kv_map),
        ],
        out_specs=pl.BlockSpec((qs, Q_PER_KV * HEAD_DIM), lambda b, h, si: (b * nsuper + si, h)),
        out_shape=jax.ShapeDtypeStruct((n, ATTN_WIDTH), BF16),
        scratch_shapes=[
            pltpu.VMEM((m, tk), F32),
            pltpu.VMEM((m, tk), F32),
            pltpu.VMEM((m, 1), F32),
            pltpu.VMEM((m, 2 * HEAD_DIM), F32),
        ],
        compiler_params=pltpu.CompilerParams(
            dimension_semantics=("arbitrary",) * 3, vmem_limit_bytes=VMEM_LIMIT_BYTES),
        name="attention",
    )(q, k, v)


def _conv_kernel(prev_ref, main_ref, next_ref, w_ref, b_ref, lg_ref, lb_ref, o_ref,
                 ext_scr, sh_scr, y_scr, *, ts, tiles_per_seq):
    si = pl.program_id(0) % tiles_per_seq
    halo = HALO_ROWS
    lanes = HEAD_DIM
    nblk = CONV_WIDTH // lanes
    span = ts + 2 * halo - SUBLANES
    base = halo - CONV_PAD
    rows = ROWS_CONV
    groups = rows // SUBLANES
    has_prev = si > 0
    has_next = si < tiles_per_seq - 1

    for cb in range(nblk):
        cs = slice(cb * lanes, (cb + 1) * lanes)
        ext_scr[cb, 0:halo, :] = jnp.where(has_prev, prev_ref[:, cs].astype(F32), 0.0)
        ext_scr[cb, halo:halo + ts, :] = main_ref[:, cs].astype(F32)
        ext_scr[cb, halo + ts:halo + ts + halo, :] = jnp.where(has_next, next_ref[:, cs].astype(F32), 0.0)
        for res in range(SUBLANES):
            sh_scr[cb, res] = ext_scr[cb, res:res + span, :]
        taps = [jnp.broadcast_to(w_ref[j:j + 1, cs], (SUBLANES, lanes)) for j in range(CONV_TAPS)]
        bias = jnp.broadcast_to(b_ref[:, cs], (SUBLANES, lanes))

        def row_block(rb, carry, cb=cb, taps=taps, bias=bias):
            r0 = pl.multiple_of(rb * rows, rows)
            accs = [bias] * groups
            for j in range(CONV_TAPS):
                off = base + j
                win = sh_scr[cb, off % SUBLANES, pl.ds(r0 + (off // SUBLANES) * SUBLANES, rows), :]
                for r in range(groups):
                    accs[r] = accs[r] + win[r * SUBLANES:(r + 1) * SUBLANES] * taps[j]
            y_scr[cb, pl.ds(r0, rows), :] = jnp.concatenate(accs, axis=0)
            return carry

        lax.fori_loop(0, ts // rows, row_block, 0)

    def plane(cb):
        return y_scr[cb, 0:ts, :]

    tot = plane(0)
    for cb in range(1, nblk):
        tot = tot + plane(cb)
    mu = jnp.sum(tot, axis=-1, keepdims=True) * (1.0 / CONV_WIDTH)
    sq = None
    for cb in range(nblk):
        d = plane(cb) - mu
        sq = d * d if sq is None else sq + d * d
    var = jnp.sum(sq, axis=-1, keepdims=True) * (1.0 / CONV_WIDTH)
    inv = lax.rsqrt(var + EPS)
    for cb in range(nblk):
        cs = slice(cb * lanes, (cb + 1) * lanes)
        z = (plane(cb) - mu) * inv * lg_ref[:, cs] + lb_ref[:, cs]
        o_ref[:, cs] = (z * _sigmoid(z)).astype(BF16)


def _conv_module(u, seq, dw_w, dw_b, ln_g, ln_b):
    n = u.shape[0]
    ts = TS_CONV
    tiles_per_seq = seq // ts
    hb = ts // HALO_ROWS
    last = n // HALO_ROWS - 1
    span = ts + 2 * HALO_ROWS - SUBLANES
    nblk = CONV_WIDTH // HEAD_DIM
    return pl.pallas_call(
        functools.partial(_conv_kernel, ts=ts, tiles_per_seq=tiles_per_seq),
        grid=(n // ts,),
        in_specs=[
            pl.BlockSpec((HALO_ROWS, CONV_WIDTH), lambda i: (jnp.maximum(i * hb - 1, 0), 0)),
            pl.BlockSpec((ts, CONV_WIDTH), lambda i: (i, 0)),
            pl.BlockSpec((HALO_ROWS, CONV_WIDTH), lambda i: (jnp.minimum((i + 1) * hb, last), 0)),
            _const_spec((CONV_TAPS, CONV_WIDTH)),
            _const_spec((1, CONV_WIDTH)),
            _const_spec((1, CONV_WIDTH)),
            _const_spec((1, CONV_WIDTH)),
        ],
        out_specs=pl.BlockSpec((ts, CONV_WIDTH), lambda i: (i, 0)),
        out_shape=jax.ShapeDtypeStruct((n, CONV_WIDTH), BF16),
        scratch_shapes=[
            pltpu.VMEM((nblk, ts + 2 * HALO_ROWS, HEAD_DIM), F32),
            pltpu.VMEM((nblk, SUBLANES, span, HEAD_DIM), F32),
            pltpu.VMEM((nblk, ts + SUBLANES, HEAD_DIM), F32),
        ],
        compiler_params=pltpu.CompilerParams(
            dimension_semantics=("arbitrary",), vmem_limit_bytes=VMEM_LIMIT_BYTES),
        name="conv_module",
    )(u, u, u, dw_w, dw_b, ln_g, ln_b)


def _ffn_kernel(x_ref, a_ref, c_ref, wo_ref, g_ref, wg_ref, wu_ref, wd_ref, o_ref, h_scr):
    f = pl.program_id(1)

    @pl.when(f == 0)
    def _():
        mix = jnp.dot(a_ref[...], wo_ref[0:ATTN_WIDTH, :], preferred_element_type=F32)
        mix = mix + jnp.dot(c_ref[...], wo_ref[ATTN_WIDTH:, :], preferred_element_type=F32)
        x1 = x_ref[...] + mix
        o_ref[...] = x1
        ms = jnp.mean(x1 * x1, axis=-1, keepdims=True)
        h_scr[...] = (x1 * lax.rsqrt(ms + EPS) * g_ref[...]).astype(BF16)

    h = h_scr[...]
    g = jnp.dot(h, wg_ref[...], preferred_element_type=F32)
    u = jnp.dot(h, wu_ref[...], preferred_element_type=F32)
    act = (g * _sigmoid(g) * u).astype(BF16)
    o_ref[...] += jnp.dot(act, wd_ref[...], preferred_element_type=F32)


def _out_ffn(x2d, attn, conv, w_out, norm_g, w_gate, w_up, w_down):
    n = x2d.shape[0]
    tm, tf = TM_FFN, TF_FFN
    return pl.pallas_call(
        _ffn_kernel,
        grid=(n // tm, D_FF // tf),
        in_specs=[
            pl.BlockSpec((tm, D_MODEL), lambda i, f: (i, 0)),
            pl.BlockSpec((tm, ATTN_WIDTH), lambda i, f: (i, 0)),
            pl.BlockSpec((tm, CONV_WIDTH), lambda i, f: (i, 0)),
            _const_spec((D_MODEL, D_MODEL)),
            _const_spec((1, D_MODEL)),
            pl.BlockSpec((D_MODEL, tf), lambda i, f: (0, f)),
            pl.BlockSpec((D_MODEL, tf), lambda i, f: (0, f)),
            pl.BlockSpec((tf, D_MODEL), lambda i, f: (f, 0)),
        ],
        out_specs=pl.BlockSpec((tm, D_MODEL), lambda i, f: (i, 0)),
        out_shape=jax.ShapeDtypeStruct((n, D_MODEL), F32),
        scratch_shapes=[pltpu.VMEM((tm, D_MODEL), BF16)],
        compiler_params=pltpu.CompilerParams(
            dimension_semantics=("arbitrary", "arbitrary"), vmem_limit_bytes=VMEM_LIMIT_BYTES),
        name="out_ffn",
    )(x2d, attn, conv, w_out, norm_g, w_gate, w_up, w_down)


def _rope_tables(seq):
    rows = seq // GRID_W
    row_ids = jnp.repeat(jnp.arange(rows), GRID_W).astype(F32)
    col_ids = jnp.tile(jnp.arange(GRID_W), rows).astype(F32)
    half = HEAD_DIM // 2
    inv_freq = ROPE_THETA ** (-jnp.arange(0, half, 2, dtype=F32) / half)
    ang_r = row_ids[:, None] * inv_freq[None, :]
    ang_c = col_ids[:, None] * inv_freq[None, :]
    ang = jnp.concatenate([ang_r, ang_r, ang_c, ang_c], axis=-1)
    cos, sin = jnp.cos(ang), jnp.sin(ang)
    first = (jnp.arange(HEAD_DIM) // (HEAD_DIM // 4)) % 2 == 0
    sin_a = jnp.where(first[None, :], -sin, 0.0)
    sin_b = jnp.where(first[None, :], 0.0, sin)
    scale = HEAD_DIM ** -0.5 * LOG2_E
    return (cos * scale, sin_a * scale, sin_b * scale, cos, sin_a, sin_b)


def _trunk(x, p):
    batch, seq, _ = x.shape
    x2d = x.reshape(batch * seq, D_MODEL)
    q, k, v, u = _in_projection(x2d, seq, p["norm_mix_g"], p["w_in"], p["b_glu"],
                                p["q_norm_g"], p["k_norm_g"], _rope_tables(seq))
    attn = _attention(q, k, v, batch, seq)
    conv = _conv_module(u, seq, p["dw_w"], p["dw_b"], p["conv_ln_g"], p["conv_ln_b"])
    y = _out_ffn(x2d, attn, conv, p["w_out"], p["norm_ffn_g"], p["w_gate"], p["w_up"], p["w_down"])
    return y.reshape(batch, seq, D_MODEL)


def _prepare(norm_mix_g, w_in, b_glu, q_norm_g, k_norm_g, dw_w, dw_b,
             conv_ln_g, conv_ln_b, w_out, norm_ffn_g, w_gate, w_up, w_down):
    assert norm_mix_g.shape[0] == 1, "single-layer trunk"
    return {
        "norm_mix_g": norm_mix_g[0][None, :],
        "w_in": w_in[0].astype(BF16),
        "b_glu": b_glu[0][None, :],
        "q_norm_g": q_norm_g[0][None, :],
        "k_norm_g": k_norm_g[0][None, :],
        "dw_w": dw_w[0],
        "dw_b": dw_b[0][None, :],
        "conv_ln_g": conv_ln_g[0][None, :],
        "conv_ln_b": conv_ln_b[0][None, :],
        "w_out": w_out[0].astype(BF16),
        "norm_ffn_g": norm_ffn_g[0][None, :],
        "w_gate": w_gate[0].astype(BF16),
        "w_up": w_up[0].astype(BF16),
        "w_down": w_down[0].astype(BF16),
    }


def kernel(x_prompt, x_sample, norm_mix_g, w_in, b_glu, q_norm_g, k_norm_g, dw_w, dw_b,
           conv_ln_g, conv_ln_b, w_out, norm_ffn_g, w_gate, w_up, w_down):
    p = _prepare(norm_mix_g, w_in, b_glu, q_norm_g, k_norm_g, dw_w, dw_b,
                 conv_ln_g, conv_ln_b, w_out, norm_ffn_g, w_gate, w_up, w_down)
    return (_trunk(x_prompt, p), _trunk(x_sample, p))
```

```python
import functools

import jax
import jax.numpy as jnp
from jax import lax
from jax.experimental import pallas as pl
from jax.experimental.pallas import tpu as pltpu

D_MODEL = 2048
GRID_W = 64
HEAD_DIM = 128
N_Q_HEADS = 8
N_KV_HEADS = 2
Q_PER_KV = N_Q_HEADS // N_KV_HEADS
ATTN_WIDTH = N_Q_HEADS * HEAD_DIM
KV_WIDTH = N_KV_HEADS * HEAD_DIM
CONV_WIDTH = D_MODEL - ATTN_WIDTH
CONV_TAPS = 31
CONV_PAD = CONV_TAPS // 2
IN_COLS = ATTN_WIDTH + 2 * KV_WIDTH + 2 * CONV_WIDTH
D_FF = 5632
ROPE_THETA = 10000.0
EPS = 1e-6
LOG2_E = 1.4426950408889634

F32 = jnp.float32
BF16 = jnp.bfloat16

VMEM_LIMIT_BYTES = 56 * 1024 * 1024
SUBLANES = 8
BF16_ROWS = 16
HALO_ROWS = BF16_ROWS

TM_PROJ = 512
TQ_ATTN = 256
TK_ATTN = 2048
QS_ATTN = 4096
TS_CONV = 256
ROWS_CONV = 64
TM_FFN = 512
TF_FFN = 512


def _const_spec(shape):
    return pl.BlockSpec(shape, lambda *_: (0,) * len(shape), pipeline_mode=pl.Buffered(1))


def _sigmoid(x):
    return 1.0 / (1.0 + jnp.exp(-x))


def _inproj_kernel(x_ref, g_ref, w_ref, b_ref, qg_ref, kg_ref,
                   cq_ref, sqa_ref, sqb_ref, ck_ref, ska_ref, skb_ref,
                   q_ref, k_ref, v_ref, u_ref, h_scr):
    x = x_ref[...]
    ms = jnp.mean(x * x, axis=-1, keepdims=True)
    h_scr[...] = (x * lax.rsqrt(ms + EPS) * g_ref[...]).astype(BF16)
    h = h_scr[...]

    def norm_rope(z, gain, c, sa, sb):
        z = z * lax.rsqrt(jnp.mean(z * z, axis=-1, keepdims=True) + EPS) * gain
        fwd = pltpu.roll(z, HEAD_DIM - HEAD_DIM // 4, 1)
        bwd = pltpu.roll(z, HEAD_DIM // 4, 1)
        return z * c + fwd * sa + bwd * sb

    cq, sqa, sqb = cq_ref[...], sqa_ref[...], sqb_ref[...]
    ck, ska, skb = ck_ref[...], ska_ref[...], skb_ref[...]
    qg, kg = qg_ref[...], kg_ref[...]

    half = ATTN_WIDTH // 2
    for c in range(2):
        z = jnp.dot(h, w_ref[:, c * half:(c + 1) * half], preferred_element_type=F32)
        for i in range(half // HEAD_DIM):
            zi = z[:, i * HEAD_DIM:(i + 1) * HEAD_DIM]
            q_ref[c * (half // HEAD_DIM) + i] = norm_rope(zi, qg, cq, sqa, sqb).astype(BF16)

    z = jnp.dot(h, w_ref[:, ATTN_WIDTH:ATTN_WIDTH + 2 * KV_WIDTH], preferred_element_type=F32)
    for i in range(N_KV_HEADS):
        zi = z[:, i * HEAD_DIM:(i + 1) * HEAD_DIM]
        k_ref[i] = norm_rope(zi, kg, ck, ska, skb).astype(BF16)
        v_ref[i, :, 0:HEAD_DIM] = z[:, KV_WIDTH + i * HEAD_DIM:KV_WIDTH + (i + 1) * HEAD_DIM].astype(BF16)
        v_ref[i, :, HEAD_DIM:] = jnp.ones((x.shape[0], HEAD_DIM), BF16)

    u0 = ATTN_WIDTH + 2 * KV_WIDTH
    cw = CONV_WIDTH // 2
    for c in range(2):
        a = jnp.dot(h, w_ref[:, u0 + c * cw:u0 + (c + 1) * cw], preferred_element_type=F32)
        a = a + b_ref[:, c * cw:(c + 1) * cw]
        gt = jnp.dot(h, w_ref[:, u0 + CONV_WIDTH + c * cw:u0 + CONV_WIDTH + (c + 1) * cw],
                     preferred_element_type=F32)
        gt = gt + b_ref[:, CONV_WIDTH + c * cw:CONV_WIDTH + (c + 1) * cw]
        u_ref[:, c * cw:(c + 1) * cw] = (a * _sigmoid(gt)).astype(BF16)


def _in_projection(x2d, seq, norm_g, w_in, b_glu, q_g, k_g, tables):
    n = x2d.shape[0]
    tm = TM_PROJ
    tiles_per_seq = seq // tm
    row = lambda i: (i, 0)
    tab = lambda i: (i % tiles_per_seq, 0)
    head_major = lambda i: (0, i, 0)
    tab_spec = pl.BlockSpec((tm, HEAD_DIM), tab)
    return pl.pallas_call(
        _inproj_kernel,
        grid=(n // tm,),
        in_specs=[
            pl.BlockSpec((tm, D_MODEL), row),
            _const_spec((1, D_MODEL)),
            _const_spec((D_MODEL, IN_COLS)),
            _const_spec((1, 2 * CONV_WIDTH)),
            _const_spec((1, HEAD_DIM)),
            _const_spec((1, HEAD_DIM)),
            tab_spec, tab_spec, tab_spec, tab_spec, tab_spec, tab_spec,
        ],
        out_specs=[
            pl.BlockSpec((N_Q_HEADS, tm, HEAD_DIM), head_major),
            pl.BlockSpec((N_KV_HEADS, tm, HEAD_DIM), head_major),
            pl.BlockSpec((N_KV_HEADS, tm, 2 * HEAD_DIM), head_major),
            pl.BlockSpec((tm, CONV_WIDTH), row),
        ],
        out_shape=[
            jax.ShapeDtypeStruct((N_Q_HEADS, n, HEAD_DIM), BF16),
            jax.ShapeDtypeStruct((N_KV_HEADS, n, HEAD_DIM), BF16),
            jax.ShapeDtypeStruct((N_KV_HEADS, n, 2 * HEAD_DIM), BF16),
            jax.ShapeDtypeStruct((n, CONV_WIDTH), BF16),
        ],
        scratch_shapes=[pltpu.VMEM((tm, D_MODEL), BF16)],
        compiler_params=pltpu.CompilerParams(
            dimension_semantics=("arbitrary",), vmem_limit_bytes=VMEM_LIMIT_BYTES),
        name="in_projection",
    )(x2d, norm_g, w_in, b_glu, q_g, k_g, *tables)


def _attn_kernel(q_ref, k_ref, v_ref, o_ref, s0, s1, m_scr, acc_scr, *, tq, tk, nq, nk):
    m_rows = Q_PER_KV * tq
    n_pairs = nq * nk
    s_buf = (s0, s1)

    def split(t):
        t = jnp.minimum(t, n_pairs - 1)
        return t // nk, t % nk

    def stage_a(t, par):
        qi, kj = split(t)
        q = q_ref[:, pl.ds(pl.multiple_of(qi * tq, tq), tq), :].reshape(m_rows, HEAD_DIM)
        k = k_ref[0, pl.ds(pl.multiple_of(kj * tk, tk), tk), :]
        s_buf[par][...] = lax.dot_general(q, k, (((1,), (1,)), ((), ())),
                                          preferred_element_type=F32)

    def stage_b(t, par):
        qi, kj = split(t)
        v = v_ref[0, pl.ds(pl.multiple_of(kj * tk, tk), tk), :]
        rows = pl.ds(pl.multiple_of(qi * tq, tq), tq)
        for g in range(Q_PER_KV):
            head = slice(g * tq, (g + 1) * tq)
            tile_max = jnp.max(s_buf[par][head, :], axis=1, keepdims=True)
            m_prev = jnp.where(kj == 0, -jnp.inf, m_scr[head, :])
            m_new = jnp.maximum(m_prev, tile_max)
            alpha = jnp.exp2(m_prev - m_new)
            m_scr[head, :] = m_new
            p = jnp.exp2(s_buf[par][head, :] - m_new).astype(BF16)
            acc = acc_scr[head, :] * alpha
            acc = acc + jnp.dot(p, v, preferred_element_type=F32)
            acc_scr[head, :] = acc
            o = acc[:, 0:HEAD_DIM] / acc[:, HEAD_DIM:HEAD_DIM + 1]
            o_ref[rows, g * HEAD_DIM:(g + 1) * HEAD_DIM] = o.astype(BF16)

    m_scr[...] = jnp.zeros(m_scr.shape, F32)
    acc_scr[...] = jnp.zeros(acc_scr.shape, F32)
    stage_a(0, 0)

    def body(i, carry):
        for par in range(2):
            t = 2 * i + par
            stage_a(t + 1, 1 - par)
            stage_b(t, par)
        return carry

    lax.fori_loop(0, n_pairs // 2, body, 0)


def _attention(q, k, v, batch, seq):
    n = batch * seq
    tq = TQ_ATTN
    tk = min(TK_ATTN, seq)
    qs = min(QS_ATTN, seq)
    nq, nk, nsuper = qs // tq, seq // tk, seq // qs
    assert (nq * nk) % 2 == 0
    m = Q_PER_KV * tq
    kv_map = lambda b, h, si: (h, b, 0)
    return pl.pallas_call(
        functools.partial(_attn_kernel, tq=tq, tk=tk, nq=nq, nk=nk),
        grid=(batch, N_KV_HEADS, nsuper),
        in_specs=[
            pl.BlockSpec((Q_PER_KV, qs, HEAD_DIM), lambda b, h, si: (h, b * nsuper + si, 0)),
            pl.BlockSpec((1, seq, HEAD_DIM), kv_map),
            pl.BlockSpec((1, seq, 2 * HEAD_DIM), kv_map),
        ],
        out_specs=pl.BlockSpec((qs, Q_PER_KV * HEAD_DIM), lambda b, h, si: (b * nsuper + si, h)),
        out_shape=jax.ShapeDtypeStruct((n, ATTN_WIDTH), BF16),
        scratch_shapes=[
            pltpu.VMEM((m, tk), F32),
            pltpu.VMEM((m, tk), F32),
            pltpu.VMEM((m, 1), F32),
            pltpu.VMEM((m, 2 * HEAD_DIM), F32),
        ],
        compiler_params=pltpu.CompilerParams(
            dimension_semantics=("arbitrary",) * 3, vmem_limit_bytes=VMEM_LIMIT_BYTES),
        name="attention",
    )(q, k, v)


def _conv_kernel(prev_ref, main_ref, next_ref, w_ref, b_ref, lg_ref, lb_ref, o_ref,
                 ext_scr, sh_scr, y_scr, *, ts, tiles_per_seq):
    si = pl.program_id(0) % tiles_per_seq
    halo = HALO_ROWS
    lanes = HEAD_DIM
    nblk = CONV_WIDTH // lanes
    span = ts + 2 * halo - SUBLANES
    base = halo - CONV_PAD
    rows = ROWS_CONV
    groups = rows // SUBLANES
    has_prev = si > 0
    has_next = si < tiles_per_seq - 1

    for cb in range(nblk):
        cs = slice(cb * lanes, (cb + 1) * lanes)
        ext_scr[cb, 0:halo, :] = jnp.where(has_prev, prev_ref[:, cs].astype(F32), 0.0)
        ext_scr[cb, halo:halo + ts, :] = main_ref[:, cs].astype(F32)
        ext_scr[cb, halo + ts:halo + ts + halo, :] = jnp.where(has_next, next_ref[:, cs].astype(F32), 0.0)
        for res in range(SUBLANES):
            sh_scr[cb, res] = ext_scr[cb, res:res + span, :]
        taps = [jnp.broadcast_to(w_ref[j:j + 1, cs], (SUBLANES, lanes)) for j in range(CONV_TAPS)]
        bias = jnp.broadcast_to(b_ref[:, cs], (SUBLANES, lanes))

        def row_block(rb, carry, cb=cb, taps=taps, bias=bias):
            r0 = pl.multiple_of(rb * rows, rows)
            accs = [bias] * groups
            for j in range(CONV_TAPS):
                off = base + j
                win = sh_scr[cb, off % SUBLANES, pl.ds(r0 + (off // SUBLANES) * SUBLANES, rows), :]
                for r in range(groups):
                    accs[r] = accs[r] + win[r * SUBLANES:(r + 1) * SUBLANES] * taps[j]
            y_scr[cb, pl.ds(r0, rows), :] = jnp.concatenate(accs, axis=0)
            return carry

        lax.fori_loop(0, ts // rows, row_block, 0)

    def plane(cb):
        return y_scr[cb, 0:ts, :]

    tot = plane(0)
    for cb in range(1, nblk):
        tot = tot + plane(cb)
    mu = jnp.sum(tot, axis=-1, keepdims=True) * (1.0 / CONV_WIDTH)
    sq = None
    for cb in range(nblk):
        d = plane(cb) - mu
        sq = d * d if sq is None else sq + d * d
    var = jnp.sum(sq, axis=-1, keepdims=True) * (1.0 / CONV_WIDTH)
    inv = lax.rsqrt(var + EPS)
    for cb in range(nblk):
        cs = slice(cb * lanes, (cb + 1) * lanes)
        z = (plane(cb) - mu) * inv * lg_ref[:, cs] + lb_ref[:, cs]
        o_ref[:, cs] = (z * _sigmoid(z)).astype(BF16)


def _conv_module(u, seq, dw_w, dw_b, ln_g, ln_b):
    n = u.shape[0]
    ts = TS_CONV
    tiles_per_seq = seq // ts
    hb = ts // HALO_ROWS
    last = n // HALO_ROWS - 1
    span = ts + 2 * HALO_ROWS - SUBLANES
    nblk = CONV_WIDTH // HEAD_DIM
    return pl.pallas_call(
        functools.partial(_conv_kernel, ts=ts, tiles_per_seq=tiles_per_seq),
        grid=(n // ts,),
        in_specs=[
            pl.BlockSpec((HALO_ROWS, CONV_WIDTH), lambda i: (jnp.maximum(i * hb - 1, 0), 0)),
            pl.BlockSpec((ts, CONV_WIDTH), lambda i: (i, 0)),
            pl.BlockSpec((HALO_ROWS, CONV_WIDTH), lambda i: (jnp.minimum((i + 1) * hb, last), 0)),
            _const_spec((CONV_TAPS, CONV_WIDTH)),
            _const_spec((1, CONV_WIDTH)),
            _const_spec((1, CONV_WIDTH)),
            _const_spec((1, CONV_WIDTH)),
        ],
        out_specs=pl.BlockSpec((ts, CONV_WIDTH), lambda i: (i, 0)),
        out_shape=jax.ShapeDtypeStruct((n, CONV_WIDTH), BF16),
        scratch_shapes=[
            pltpu.VMEM((nblk, ts + 2 * HALO_ROWS, HEAD_DIM), F32),
            pltpu.VMEM((nblk, SUBLANES, span, HEAD_DIM), F32),
            pltpu.VMEM((nblk, ts + SUBLANES, HEAD_DIM), F32),
        ],
        compiler_params=pltpu.CompilerParams(
            dimension_semantics=("arbitrary",), vmem_limit_bytes=VMEM_LIMIT_BYTES),
        name="conv_module",
    )(u, u, u, dw_w, dw_b, ln_g, ln_b)


def _ffn_kernel(x_ref, a_ref, c_ref, wo_ref, g_ref, wg_ref, wu_ref, wd_ref, o_ref, h_scr):
    f = pl.program_id(1)

    @pl.when(f == 0)
    def _():
        mix = jnp.dot(a_ref[...], wo_ref[0:ATTN_WIDTH, :], preferred_element_type=F32)
        mix = mix + jnp.dot(c_ref[...], wo_ref[ATTN_WIDTH:, :], preferred_element_type=F32)
        x1 = x_ref[...] + mix
        o_ref[...] = x1
        ms = jnp.mean(x1 * x1, axis=-1, keepdims=True)
        h_scr[...] = (x1 * lax.rsqrt(ms + EPS) * g_ref[...]).astype(BF16)

    h = h_scr[...]
    g = jnp.dot(h, wg_ref[...], preferred_element_type=F32)
    u = jnp.dot(h, wu_ref[...], preferred_element_type=F32)
    act = (g * _sigmoid(g) * u).astype(BF16)
    o_ref[...] += jnp.dot(act, wd_ref[...], preferred_element_type=F32)


def _out_ffn(x2d, attn, conv, w_out, norm_g, w_gate, w_up, w_down):
    n = x2d.shape[0]
    tm, tf = TM_FFN, TF_FFN
    return pl.pallas_call(
        _ffn_kernel,
        grid=(n // tm, D_FF // tf),
        in_specs=[
            pl.BlockSpec((tm, D_MODEL), lambda i, f: (i, 0)),
            pl.BlockSpec((tm, ATTN_WIDTH), lambda i, f: (i, 0)),
            pl.BlockSpec((tm, CONV_WIDTH), lambda i, f: (i, 0)),
            _const_spec((D_MODEL, D_MODEL)),
            _const_spec((1, D_MODEL)),
            pl.BlockSpec((D_MODEL, tf), lambda i, f: (0, f)),
            pl.BlockSpec((D_MODEL, tf), lambda i, f: (0, f)),
            pl.BlockSpec((tf, D_MODEL), lambda i, f: (f, 0)),
        ],
        out_specs=pl.BlockSpec((tm, D_MODEL), lambda i, f: (i, 0)),
        out_shape=jax.ShapeDtypeStruct((n, D_MODEL), F32),
        scratch_shapes=[pltpu.VMEM((tm, D_MODEL), BF16)],
        compiler_params=pltpu.CompilerParams(
            dimension_semantics=("arbitrary", "arbitrary"), vmem_limit_bytes=VMEM_LIMIT_BYTES),
        name="out_ffn",
    )(x2d, attn, conv, w_out, norm_g, w_gate, w_up, w_down)


def _rope_tables(seq):
    rows = seq // GRID_W
    row_ids = jnp.repeat(jnp.arange(rows), GRID_W).astype(F32)
    col_ids = jnp.tile(jnp.arange(GRID_W), rows).astype(F32)
    half = HEAD_DIM // 2
    inv_freq = ROPE_THETA ** (-jnp.arange(0, half, 2, dtype=F32) / half)
    ang_r = row_ids[:, None] * inv_freq[None, :]
    ang_c = col_ids[:, None] * inv_freq[None, :]
    ang = jnp.concatenate([ang_r, ang_r, ang_c, ang_c], axis=-1)
    cos, sin = jnp.cos(ang), jnp.sin(ang)
    first = (jnp.arange(HEAD_DIM) // (HEAD_DIM // 4)) % 2 == 0
    sin_a = jnp.where(first[None, :], -sin, 0.0)
    sin_b = jnp.where(first[None, :], 0.0, sin)
    scale = HEAD_DIM ** -0.5 * LOG2_E
    return (cos * scale, sin_a * scale, sin_b * scale, cos, sin_a, sin_b)


def _trunk(x, p):
    batch, seq, _ = x.shape
    x2d = x.reshape(batch * seq, D_MODEL)
    q, k, v, u = _in_projection(x2d, seq, p["norm_mix_g"], p["w_in"], p["b_glu"],
                                p["q_norm_g"], p["k_norm_g"], _rope_tables(seq))
    attn = _attention(q, k, v, batch, seq)
    conv = _conv_module(u, seq, p["dw_w"], p["dw_b"], p["conv_ln_g"], p["conv_ln_b"])
    y = _out_ffn(x2d, attn, conv, p["w_out"], p["norm_ffn_g"], p["w_gate"], p["w_up"], p["w_down"])
    return y.reshape(batch, seq, D_MODEL)


def _prepare(norm_mix_g, w_in, b_glu, q_norm_g, k_norm_g, dw_w, dw_b,
             conv_ln_g, conv_ln_b, w_out, norm_ffn_g, w_gate, w_up, w_down):
    assert norm_mix_g.shape[0] == 1, "single-layer trunk"
    return {
        "norm_mix_g": norm_mix_g[0][None, :],
        "w_in": w_in[0].astype(BF16),
        "b_glu": b_glu[0][None, :],
        "q_norm_g": q_norm_g[0][None, :],
        "k_norm_g": k_norm_g[0][None, :],
        "dw_w": dw_w[0],
        "dw_b": dw_b[0][None, :],
        "conv_ln_g": conv_ln_g[0][None, :],
        "conv_ln_b": conv_ln_b[0][None, :],
        "w_out": w_out[0].astype(BF16),
        "norm_ffn_g": norm_ffn_g[0][None, :],
        "w_gate": w_gate[0].astype(BF16),
        "w_up": w_up[0].astype(BF16),
        "w_down": w_down[0].astype(BF16),
    }


def kernel(x_prompt, x_sample, norm_mix_g, w_in, b_glu, q_norm_g, k_norm_g, dw_w, dw_b,
           conv_ln_g, conv_ln_b, w_out, norm_ffn_g, w_gate, w_up, w_down):
    p = _prepare(norm_mix_g, w_in, b_glu, q_norm_g, k_norm_g, dw_w, dw_b,
                 conv_ln_g, conv_ln_b, w_out, norm_ffn_g, w_gate, w_up, w_down)
    return (_trunk(x_prompt, p), _trunk(x_sample, p))
```

```python
import functools

import jax
import jax.numpy as jnp
from jax import lax
from jax.experimental import pallas as pl
from jax.experimental.pallas import tpu as pltpu

D_MODEL = 2048
GRID_W = 64
HEAD_DIM = 128
N_Q_HEADS = 8
N_KV_HEADS = 2
Q_PER_KV = N_Q_HEADS // N_KV_HEADS
ATTN_WIDTH = N_Q_HEADS * HEAD_DIM
KV_WIDTH = N_KV_HEADS * HEAD_DIM
CONV_WIDTH = D_MODEL - ATTN_WIDTH
CONV_TAPS = 31
CONV_PAD = CONV_TAPS // 2
IN_COLS = ATTN_WIDTH + 2 * KV_WIDTH + 2 * CONV_WIDTH
D_FF = 5632
ROPE_THETA = 10000.0
EPS = 1e-6
LOG2_E = 1.4426950408889634

F32 = jnp.float32
BF16 = jnp.bfloat16

VMEM_LIMIT_BYTES = 56 * 1024 * 1024
SUBLANES = 8
BF16_ROWS = 16
HALO_ROWS = BF16_ROWS

TM_PROJ = 512
TQ_ATTN = 256
TK_ATTN = 2048
QS_ATTN = 4096
TS_CONV = 512
ROWS_CONV = 128
TM_FFN = 1024
TF_FFN = 512
OUT_PROJ_CHUNKS = 4


def _const_spec(shape):
    return pl.BlockSpec(shape, lambda *_: (0,) * len(shape), pipeline_mode=pl.Buffered(1))


def _sigmoid(x):
    return 1.0 / (1.0 + jnp.exp(-x))


def _inproj_kernel(x_ref, g_ref, w_ref, b_ref, qg_ref, kg_ref,
                   cq_ref, sqa_ref, sqb_ref, ck_ref, ska_ref, skb_ref,
                   q_ref, k_ref, v_ref, u_ref, h_scr):
    x = x_ref[...]
    ms = jnp.mean(x * x, axis=-1, keepdims=True)
    h_scr[...] = (x * lax.rsqrt(ms + EPS) * g_ref[...]).astype(BF16)
    h = h_scr[...]

    def norm_rope(z, gain, c, sa, sb):
        z = z * lax.rsqrt(jnp.mean(z * z, axis=-1, keepdims=True) + EPS) * gain
        fwd = pltpu.roll(z, HEAD_DIM - HEAD_DIM // 4, 1)
        bwd = pltpu.roll(z, HEAD_DIM // 4, 1)
        return z * c + fwd * sa + bwd * sb

    cq, sqa, sqb = cq_ref[...], sqa_ref[...], sqb_ref[...]
    ck, ska, skb = ck_ref[...], ska_ref[...], skb_ref[...]
    qg, kg = qg_ref[...], kg_ref[...]

    half = ATTN_WIDTH // 2
    for c in range(2):
        z = jnp.dot(h, w_ref[:, c * half:(c + 1) * half], preferred_element_type=F32)
        for i in range(half // HEAD_DIM):
            zi = z[:, i * HEAD_DIM:(i + 1) * HEAD_DIM]
            q_ref[c * (half // HEAD_DIM) + i] = norm_rope(zi, qg, cq, sqa, sqb).astype(BF16)

    z = jnp.dot(h, w_ref[:, ATTN_WIDTH:ATTN_WIDTH + 2 * KV_WIDTH], preferred_element_type=F32)
    for i in range(N_KV_HEADS):
        zi = z[:, i * HEAD_DIM:(i + 1) * HEAD_DIM]
        k_ref[i] = norm_rope(zi, kg, ck, ska, skb).astype(BF16)
        v_ref[i, :, 0:HEAD_DIM] = z[:, KV_WIDTH + i * HEAD_DIM:KV_WIDTH + (i + 1) * HEAD_DIM].astype(BF16)
        v_ref[i, :, HEAD_DIM:] = jnp.ones((x.shape[0], HEAD_DIM), BF16)

    u0 = ATTN_WIDTH + 2 * KV_WIDTH
    cw = CONV_WIDTH // 2
    for c in range(2):
        a = jnp.dot(h, w_ref[:, u0 + c * cw:u0 + (c + 1) * cw], preferred_element_type=F32)
        a = a + b_ref[:, c * cw:(c + 1) * cw]
        gt = jnp.dot(h, w_ref[:, u0 + CONV_WIDTH + c * cw:u0 + CONV_WIDTH + (c + 1) * cw],
                     preferred_element_type=F32)
        gt = gt + b_ref[:, CONV_WIDTH + c * cw:CONV_WIDTH + (c + 1) * cw]
        u_ref[:, c * cw:(c + 1) * cw] = (a * _sigmoid(gt)).astype(BF16)


def _in_projection(x2d, seq, norm_g, w_in, b_glu, q_g, k_g, tables):
    n = x2d.shape[0]
    tm = TM_PROJ
    tiles_per_seq = seq // tm
    row = lambda i: (i, 0)
    tab = lambda i: (i % tiles_per_seq, 0)
    head_major = lambda i: (0, i, 0)
    tab_spec = pl.BlockSpec((tm, HEAD_DIM), tab)
    return pl.pallas_call(
        _inproj_kernel,
        grid=(n // tm,),
        in_specs=[
            pl.BlockSpec((tm, D_MODEL), row),
            _const_spec((1, D_MODEL)),
            _const_spec((D_MODEL, IN_COLS)),
            _const_spec((1, 2 * CONV_WIDTH)),
            _const_spec((1, HEAD_DIM)),
            _const_spec((1, HEAD_DIM)),
            tab_spec, tab_spec, tab_spec, tab_spec, tab_spec, tab_spec,
        ],
        out_specs=[
            pl.BlockSpec((N_Q_HEADS, tm, HEAD_DIM), head_major),
            pl.BlockSpec((N_KV_HEADS, tm, HEAD_DIM), head_major),
            pl.BlockSpec((N_KV_HEADS, tm, 2 * HEAD_DIM), head_major),
            pl.BlockSpec((tm, CONV_WIDTH), row),
        ],
        out_shape=[
            jax.ShapeDtypeStruct((N_Q_HEADS, n, HEAD_DIM), BF16),
            jax.ShapeDtypeStruct((N_KV_HEADS, n, HEAD_DIM), BF16),
            jax.ShapeDtypeStruct((N_KV_HEADS, n, 2 * HEAD_DIM), BF16),
            jax.ShapeDtypeStruct((n, CONV_WIDTH), BF16),
        ],
        scratch_shapes=[pltpu.VMEM((tm, D_MODEL), BF16)],
        compiler_params=pltpu.CompilerParams(
            dimension_semantics=("arbitrary",), vmem_limit_bytes=VMEM_LIMIT_BYTES),
        name="in_projection",
    )(x2d, norm_g, w_in, b_glu, q_g, k_g, *tables)


def _attn_kernel(q_ref, k_ref, v_ref, o_ref, s0, s1, m_scr, acc_scr, *, tq, tk, nq, nk):
    m_rows = Q_PER_KV * tq
    n_pairs = nq * nk
    s_buf = (s0, s1)

    def split(t):
        t = jnp.minimum(t, n_pairs - 1)
        return t // nk, t % nk

    def stage_a(t, par):
        qi, kj = split(t)
        q = q_ref[:, pl.ds(pl.multiple_of(qi * tq, tq), tq), :].reshape(m_rows, HEAD_DIM)
        k = k_ref[0, pl.ds(pl.multiple_of(kj * tk, tk), tk), :]
        s_buf[par][...] = lax.dot_general(q, k, (((1,), (1,)), ((), ())),
                                          preferred_element_type=F32)

    def stage_b(t, par):
        qi, kj = split(t)
        v = v_ref[0, pl.ds(pl.multiple_of(kj * tk, tk), tk), :]
        rows = pl.ds(pl.multiple_of(qi * tq, tq), tq)
        for g in range(Q_PER_KV):
            head = slice(g * tq, (g + 1) * tq)
            tile_max = jnp.max(s_buf[par][head, :], axis=1, keepdims=True)
            m_prev = jnp.where(kj == 0, -jnp.inf, m_scr[head, :])
            m_new = jnp.maximum(m_prev, tile_max)
            alpha = jnp.exp2(m_prev - m_new)
            m_scr[head, :] = m_new
            p = jnp.exp2(s_buf[par][head, :] - m_new).astype(BF16)
            acc = acc_scr[head, :] * alpha
            acc = acc + jnp.dot(p, v, preferred_element_type=F32)
            acc_scr[head, :] = acc
            o = acc[:, 0:HEAD_DIM] / acc[:, HEAD_DIM:HEAD_DIM + 1]
            o_ref[rows, g * HEAD_DIM:(g + 1) * HEAD_DIM] = o.astype(BF16)

    m_scr[...] = jnp.zeros(m_scr.shape, F32)
    acc_scr[...] = jnp.zeros(acc_scr.shape, F32)
    stage_a(0, 0)

    def body(i, carry):
        for par in range(2):
            t = 2 * i + par
            stage_a(t + 1, 1 - par)
            stage_b(t, par)
        return carry

    lax.fori_loop(0, n_pairs // 2, body, 0)


def _attention(q, k, v, batch, seq):
    n = batch * seq
    tq = TQ_ATTN
    tk = min(TK_ATTN, seq)
    qs = min(QS_ATTN, seq)
    nq, nk, nsuper = qs // tq, seq // tk, seq // qs
    assert (nq * nk) % 2 == 0
    m = Q_PER_KV * tq
    kv_map = lambda b, h, si: (h, b, 0)
    return pl.pallas_call(
        functools.partial(_attn_kernel, tq=tq, tk=tk, nq=nq, nk=nk),
        grid=(batch, N_KV_HEADS, nsuper),
        in_specs=[
            pl.BlockSpec((Q_PER_KV, qs, HEAD_DIM), lambda b, h, si: (h, b * nsuper + si, 0)),
            pl.BlockSpec((1, seq, HEAD_DIM), kv_map),
            pl.BlockSpec((1, seq, 2 * HEAD_DIM), kv_map),
        ],
        out_specs=pl.BlockSpec((qs, Q_PER_KV * HEAD_DIM), lambda b, h, si: (b * nsuper + si, h)),
        out_shape=jax.ShapeDtypeStruct((n, ATTN_WIDTH), BF16),
        scratch_shapes=[
            pltpu.VMEM((m, tk), F32),
            pltpu.VMEM((m, tk), F32),
            pltpu.VMEM((m, 1), F32),
            pltpu.VMEM((m, 2 * HEAD_DIM), F32),
        ],
        compiler_params=pltpu.CompilerParams(
            dimension_semantics=("arbitrary",) * 3, vmem_limit_bytes=VMEM_LIMIT_BYTES),
        name="attention",
    )(q, k, v)


def _conv_kernel(prev_ref, main_ref, next_ref, w_ref, b_ref, lg_ref, lb_ref, o_ref,
                 ext_scr, sh_scr, y_scr, *, ts, tiles_per_seq):
    si = pl.program_id(0) % tiles_per_seq
    halo = HALO_ROWS
    lanes = HEAD_DIM
    nblk = CONV_WIDTH // lanes
    span = ts + 2 * halo - SUBLANES
    base = halo - CONV_PAD
    rows = ROWS_CONV
    groups = rows // SUBLANES
    has_prev = si > 0
    has_next = si < tiles_per_seq - 1

    for cb in range(nblk):
        cs = slice(cb * lanes, (cb + 1) * lanes)
        ext_scr[cb, 0:halo, :] = jnp.where(has_prev, prev_ref[:, cs].astype(F32), 0.0)
        ext_scr[cb, halo:halo + ts, :] = main_ref[:, cs].astype(F32)
        ext_scr[cb, halo + ts:halo + ts + halo, :] = jnp.where(has_next, next_ref[:, cs].astype(F32), 0.0)
        for res in range(1, SUBLANES):
            sh_scr[cb, res - 1] = ext_scr[cb, res:res + span, :]
        taps = [jnp.broadcast_to(w_ref[j:j + 1, cs], (SUBLANES, lanes)) for j in range(CONV_TAPS)]
        bias = jnp.broadcast_to(b_ref[:, cs], (SUBLANES, lanes))

        def row_block(rb, carry, cb=cb, taps=taps, bias=bias):
            r0 = pl.multiple_of(rb * rows, rows)
            accs = [bias] * groups
            for j in range(CONV_TAPS):
                off = base + j
                res, start = off % SUBLANES, r0 + (off // SUBLANES) * SUBLANES
                if res == 0:
                    win = ext_scr[cb, pl.ds(start, rows), :]
                else:
                    win = sh_scr[cb, res - 1, pl.ds(start, rows), :]
                for r in range(groups):
                    accs[r] = accs[r] + win[r * SUBLANES:(r + 1) * SUBLANES] * taps[j]
            y_scr[cb, pl.ds(r0, rows), :] = jnp.concatenate(accs, axis=0)
            return carry

        lax.fori_loop(0, ts // rows, row_block, 0)

    def plane(cb):
        return y_scr[cb, 0:ts, :]

    tot = plane(0)
    for cb in range(1, nblk):
        tot = tot + plane(cb)
    mu = jnp.sum(tot, axis=-1, keepdims=True) * (1.0 / CONV_WIDTH)
    sq = None
    for cb in range(nblk):
        d = plane(cb) - mu
        sq = d * d if sq is None else sq + d * d
    var = jnp.sum(sq, axis=-1, keepdims=True) * (1.0 / CONV_WIDTH)
    inv = lax.rsqrt(var + EPS)
    for cb in range(nblk):
        cs = slice(cb * lanes, (cb + 1) * lanes)
        z = (plane(cb) - mu) * inv * lg_ref[:, cs] + lb_ref[:, cs]
        o_ref[:, cs] = (z * _sigmoid(z)).astype(BF16)


def _conv_module(u, seq, dw_w, dw_b, ln_g, ln_b):
    n = u.shape[0]
    ts = TS_CONV
    tiles_per_seq = seq // ts
    hb = ts // HALO_ROWS
    last = n // HALO_ROWS - 1
    span = ts + 2 * HALO_ROWS - SUBLANES
    nblk = CONV_WIDTH // HEAD_DIM
    return pl.pallas_call(
        functools.partial(_conv_kernel, ts=ts, tiles_per_seq=tiles_per_seq),
        grid=(n // ts,),
        in_specs=[
            pl.BlockSpec((HALO_ROWS, CONV_WIDTH), lambda i: (jnp.maximum(i * hb - 1, 0), 0)),
            pl.BlockSpec((ts, CONV_WIDTH), lambda i: (i, 0)),
            pl.BlockSpec((HALO_ROWS, CONV_WIDTH), lambda i: (jnp.minimum((i + 1) * hb, last), 0)),
            _const_spec((CONV_TAPS, CONV_WIDTH)),
            _const_spec((1, CONV_WIDTH)),
            _const_spec((1, CONV_WIDTH)),
            _const_spec((1, CONV_WIDTH)),
        ],
        out_specs=pl.BlockSpec((ts, CONV_WIDTH), lambda i: (i, 0)),
        out_shape=jax.ShapeDtypeStruct((n, CONV_WIDTH), BF16),
        scratch_shapes=[
            pltpu.VMEM((nblk, ts + 2 * HALO_ROWS, HEAD_DIM), F32),
            pltpu.VMEM((nblk, SUBLANES - 1, span, HEAD_DIM), F32),
            pltpu.VMEM((nblk, ts + SUBLANES, HEAD_DIM), F32),
        ],
        compiler_params=pltpu.CompilerParams(
            dimension_semantics=("arbitrary",), vmem_limit_bytes=VMEM_LIMIT_BYTES),
        name="conv_module",
    )(u, u, u, dw_w, dw_b, ln_g, ln_b)


def _ffn_kernel(x_hbm, a_ref, c_ref, wo_ref, g_ref, wg_ref, wu_ref, wd_ref, o_ref, h_scr, x_sem, *, tm):
    i = pl.program_id(0)
    f = pl.program_id(1)

    @pl.when(f == 0)
    def _():
        x_copy = pltpu.make_async_copy(x_hbm.at[pl.ds(pl.multiple_of(i * tm, tm), tm), :], o_ref, x_sem)
        x_copy.start()
        cols = D_MODEL // OUT_PROJ_CHUNKS
        ssq = None
        for c in range(OUT_PROJ_CHUNKS):
            cs = slice(c * cols, (c + 1) * cols)
            mix = jnp.dot(a_ref[...], wo_ref[0:ATTN_WIDTH, cs], preferred_element_type=F32)
            mix = mix + jnp.dot(c_ref[...], wo_ref[ATTN_WIDTH:, cs], preferred_element_type=F32)
            if c == 0:
                x_copy.wait()
            x1 = o_ref[:, cs] + mix
            o_ref[:, cs] = x1
            part = jnp.sum(x1 * x1, axis=-1, keepdims=True)
            ssq = part if ssq is None else ssq + part
        inv = lax.rsqrt(ssq * (1.0 / D_MODEL) + EPS)
        h_scr[...] = (o_ref[...] * inv * g_ref[...]).astype(BF16)

    h = h_scr[...]
    g = jnp.dot(h, wg_ref[...], preferred_element_type=F32)
    u = jnp.dot(h, wu_ref[...], preferred_element_type=F32)
    act = (g * _sigmoid(g) * u).astype(BF16)
    o_ref[...] += jnp.dot(act, wd_ref[...], preferred_element_type=F32)


def _out_ffn(x2d, attn, conv, w_out, norm_g, w_gate, w_up, w_down):
    n = x2d.shape[0]
    tm, tf = TM_FFN, TF_FFN
    return pl.pallas_call(
        functools.partial(_ffn_kernel, tm=tm),
        grid=(n // tm, D_FF // tf),
        in_specs=[
            pl.BlockSpec(memory_space=pl.ANY),
            pl.BlockSpec((tm, ATTN_WIDTH), lambda i, f: (i, 0)),
            pl.BlockSpec((tm, CONV_WIDTH), lambda i, f: (i, 0)),
            _const_spec((D_MODEL, D_MODEL)),
            _const_spec((1, D_MODEL)),
            pl.BlockSpec((D_MODEL, tf), lambda i, f: (0, f)),
            pl.BlockSpec((D_MODEL, tf), lambda i, f: (0, f)),
            pl.BlockSpec((tf, D_MODEL), lambda i, f: (f, 0)),
        ],
        out_specs=pl.BlockSpec((tm, D_MODEL), lambda i, f: (i, 0)),
        out_shape=jax.ShapeDtypeStruct((n, D_MODEL), F32),
        scratch_shapes=[pltpu.VMEM((tm, D_MODEL), BF16), pltpu.SemaphoreType.DMA(())],
        compiler_params=pltpu.CompilerParams(
            dimension_semantics=("arbitrary", "arbitrary"), vmem_limit_bytes=VMEM_LIMIT_BYTES),
        name="out_ffn",
    )(x2d, attn, conv, w_out, norm_g, w_gate, w_up, w_down)


def _rope_tables(seq):
    rows = seq // GRID_W
    row_ids = jnp.repeat(jnp.arange(rows), GRID_W).astype(F32)
    col_ids = jnp.tile(jnp.arange(GRID_W), rows).astype(F32)
    half = HEAD_DIM // 2
    inv_freq = ROPE_THETA ** (-jnp.arange(0, half, 2, dtype=F32) / half)
    ang_r = row_ids[:, None] * inv_freq[None, :]
    ang_c = col_ids[:, None] * inv_freq[None, :]
    ang = jnp.concatenate([ang_r, ang_r, ang_c, ang_c], axis=-1)
    cos, sin = jnp.cos(ang), jnp.sin(ang)
    first = (jnp.arange(HEAD_DIM) // (HEAD_DIM // 4)) % 2 == 0
    sin_a = jnp.where(first[None, :], -sin, 0.0)
    sin_b = jnp.where(first[None, :], 0.0, sin)
    scale = HEAD_DIM ** -0.5 * LOG2_E
    return (cos * scale, sin_a * scale, sin_b * scale, cos, sin_a, sin_b)


def _trunk(x, p):
    batch, seq, _ = x.shape
    x2d = x.reshape(batch * seq, D_MODEL)
    q, k, v, u = _in_projection(x2d, seq, p["norm_mix_g"], p["w_in"], p["b_glu"],
                                p["q_norm_g"], p["k_norm_g"], _rope_tables(seq))
    attn = _attention(q, k, v, batch, seq)
    conv = _conv_module(u, seq, p["dw_w"], p["dw_b"], p["conv_ln_g"], p["conv_ln_b"])
    y = _out_ffn(x2d, attn, conv, p["w_out"], p["norm_ffn_g"], p["w_gate"], p["w_up"], p["w_down"])
    return y.reshape(batch, seq, D_MODEL)


def _prepare(norm_mix_g, w_in, b_glu, q_norm_g, k_norm_g, dw_w, dw_b,
             conv_ln_g, conv_ln_b, w_out, norm_ffn_g, w_gate, w_up, w_down):
    assert norm_mix_g.shape[0] == 1, "single-layer trunk"
    return {
        "norm_mix_g": norm_mix_g[0][None, :],
        "w_in": w_in[0].astype(BF16),
        "b_glu": b_glu[0][None, :],
        "q_norm_g": q_norm_g[0][None, :],
        "k_norm_g": k_norm_g[0][None, :],
        "dw_w": dw_w[0],
        "dw_b": dw_b[0][None, :],
        "conv_ln_g": conv_ln_g[0][None, :],
        "conv_ln_b": conv_ln_b[0][None, :],
        "w_out": w_out[0].astype(BF16),
        "norm_ffn_g": norm_ffn_g[0][None, :],
        "w_gate": w_gate[0].astype(BF16),
        "w_up": w_up[0].astype(BF16),
        "w_down": w_down[0].astype(BF16),
    }


def kernel(x_prompt, x_sample, norm_mix_g, w_in, b_glu, q_norm_g, k_norm_g, dw_w, dw_b,
           conv_ln_g, conv_ln_b, w_out, norm_ffn_g, w_gate, w_up, w_down):
    p = _prepare(norm_mix_g, w_in, b_glu, q_norm_g, k_norm_g, dw_w, dw_b,
                 conv_ln_g, conv_ln_b, w_out, norm_ffn_g, w_gate, w_up, w_down)
    return (_trunk(x_prompt, p), _trunk(x_sample, p))
```

```python
import functools

import jax
import jax.numpy as jnp
from jax import lax
from jax.experimental import pallas as pl
from jax.experimental.pallas import tpu as pltpu

D_MODEL = 2048
GRID_W = 64
HEAD_DIM = 128
N_Q_HEADS = 8
N_KV_HEADS = 2
Q_PER_KV = N_Q_HEADS // N_KV_HEADS
ATTN_WIDTH = N_Q_HEADS * HEAD_DIM
KV_WIDTH = N_KV_HEADS * HEAD_DIM
CONV_WIDTH = D_MODEL - ATTN_WIDTH
CONV_TAPS = 31
CONV_PAD = CONV_TAPS // 2
IN_COLS = ATTN_WIDTH + 2 * KV_WIDTH + 2 * CONV_WIDTH
D_FF = 5632
ROPE_THETA = 10000.0
EPS = 1e-6
LOG2_E = 1.4426950408889634

F32 = jnp.float32
BF16 = jnp.bfloat16

VMEM_LIMIT_BYTES = 56 * 1024 * 1024
SUBLANES = 8
BF16_ROWS = 16
HALO_ROWS = BF16_ROWS

TM_PROJ = 512
TQ_ATTN = 256
TK_ATTN = 2048
QS_ATTN = 4096
TS_CONV = 512
ROWS_CONV = 128
TM_FFN = 512
TF_FFN = 512


def _const_spec(shape):
    return pl.BlockSpec(shape, lambda *_: (0,) * len(shape), pipeline_mode=pl.Buffered(1))


def _sigmoid(x):
    return 1.0 / (1.0 + jnp.exp(-x))


def _inproj_kernel(x_ref, g_ref, w_ref, b_ref, qg_ref, kg_ref, cos_ref, sa_ref, sb_ref,
                   q_ref, k_ref, v_ref, u_ref, h_scr):
    x = x_ref[...]
    ms = jnp.mean(x * x, axis=-1, keepdims=True)
    h_scr[...] = (x * lax.rsqrt(ms + EPS) * g_ref[...]).astype(BF16)
    h = h_scr[...]

    def norm_rope(z, gain, c, sa, sb):
        z = z * lax.rsqrt(jnp.mean(z * z, axis=-1, keepdims=True) + EPS) * gain
        fwd = pltpu.roll(z, HEAD_DIM - HEAD_DIM // 4, 1)
        bwd = pltpu.roll(z, HEAD_DIM // 4, 1)
        return z * c + fwd * sa + bwd * sb

    cos, sa, sb = cos_ref[...], sa_ref[...], sb_ref[...]
    qg = qg_ref[...] * (HEAD_DIM ** -0.5 * LOG2_E)
    kg = kg_ref[...]

    half = ATTN_WIDTH // 2
    for c in range(2):
        z = jnp.dot(h, w_ref[:, c * half:(c + 1) * half], preferred_element_type=F32)
        for i in range(half // HEAD_DIM):
            zi = z[:, i * HEAD_DIM:(i + 1) * HEAD_DIM]
            q_ref[c * (half // HEAD_DIM) + i] = norm_rope(zi, qg, cos, sa, sb).astype(BF16)

    z = jnp.dot(h, w_ref[:, ATTN_WIDTH:ATTN_WIDTH + 2 * KV_WIDTH], preferred_element_type=F32)
    for i in range(N_KV_HEADS):
        zi = z[:, i * HEAD_DIM:(i + 1) * HEAD_DIM]
        k_ref[i] = norm_rope(zi, kg, cos, sa, sb).astype(BF16)
        v_ref[i, :, 0:HEAD_DIM] = z[:, KV_WIDTH + i * HEAD_DIM:KV_WIDTH + (i + 1) * HEAD_DIM].astype(BF16)
        v_ref[i, :, HEAD_DIM:] = jnp.ones((x.shape[0], HEAD_DIM), BF16)

    u0 = ATTN_WIDTH + 2 * KV_WIDTH
    cw = CONV_WIDTH // 2
    for c in range(2):
        a = jnp.dot(h, w_ref[:, u0 + c * cw:u0 + (c + 1) * cw], preferred_element_type=F32)
        a = a + b_ref[:, c * cw:(c + 1) * cw]
        gt = jnp.dot(h, w_ref[:, u0 + CONV_WIDTH + c * cw:u0 + CONV_WIDTH + (c + 1) * cw],
                     preferred_element_type=F32)
        gt = gt + b_ref[:, CONV_WIDTH + c * cw:CONV_WIDTH + (c + 1) * cw]
        u_ref[:, c * cw:(c + 1) * cw] = (a * _sigmoid(gt)).astype(BF16)


def _in_projection(x2d, seq, norm_g, w_in, b_glu, q_g, k_g, tables):
    n = x2d.shape[0]
    tm = TM_PROJ
    tiles_per_seq = seq // tm
    row = lambda i: (i, 0)
    tab = lambda i: (i % tiles_per_seq, 0)
    head_major = lambda i: (0, i, 0)
    tab_spec = pl.BlockSpec((tm, HEAD_DIM), tab)
    return pl.pallas_call(
        _inproj_kernel,
        grid=(n // tm,),
        in_specs=[
            pl.BlockSpec((tm, D_MODEL), row),
            _const_spec((1, D_MODEL)),
            _const_spec((D_MODEL, IN_COLS)),
            _const_spec((1, 2 * CONV_WIDTH)),
            _const_spec((1, HEAD_DIM)),
            _const_spec((1, HEAD_DIM)),
            tab_spec, tab_spec, tab_spec,
        ],
        out_specs=[
            pl.BlockSpec((N_Q_HEADS, tm, HEAD_DIM), head_major),
            pl.BlockSpec((N_KV_HEADS, tm, HEAD_DIM), head_major),
            pl.BlockSpec((N_KV_HEADS, tm, 2 * HEAD_DIM), head_major),
            pl.BlockSpec((tm, CONV_WIDTH), row),
        ],
        out_shape=[
            jax.ShapeDtypeStruct((N_Q_HEADS, n, HEAD_DIM), BF16),
            jax.ShapeDtypeStruct((N_KV_HEADS, n, HEAD_DIM), BF16),
            jax.ShapeDtypeStruct((N_KV_HEADS, n, 2 * HEAD_DIM), BF16),
            jax.ShapeDtypeStruct((n, CONV_WIDTH), BF16),
        ],
        scratch_shapes=[pltpu.VMEM((tm, D_MODEL), BF16)],
        compiler_params=pltpu.CompilerParams(
            dimension_semantics=("arbitrary",), vmem_limit_bytes=VMEM_LIMIT_BYTES),
        name="in_projection",
    )(x2d, norm_g, w_in, b_glu, q_g, k_g, *tables)


def _attn_kernel(q_ref, k_ref, v_ref, o_ref, s0, s1, m_scr, acc_scr, *, tq, tk, nq, nk):
    m_rows = Q_PER_KV * tq
    n_pairs = nq * nk
    s_buf = (s0, s1)

    def split(t):
        t = jnp.minimum(t, n_pairs - 1)
        return t // nk, t % nk

    def stage_a(t, par):
        qi, kj = split(t)
        q = q_ref[:, pl.ds(pl.multiple_of(qi * tq, tq), tq), :].reshape(m_rows, HEAD_DIM)
        k = k_ref[0, pl.ds(pl.multiple_of(kj * tk, tk), tk), :]
        s_buf[par][...] = lax.dot_general(q, k, (((1,), (1,)), ((), ())),
                                          preferred_element_type=F32)

    def stage_b(t, par):
        qi, kj = split(t)
        v = v_ref[0, pl.ds(pl.multiple_of(kj * tk, tk), tk), :]
        rows = pl.ds(pl.multiple_of(qi * tq, tq), tq)
        for g in range(Q_PER_KV):
            head = slice(g * tq, (g + 1) * tq)
            tile_max = jnp.max(s_buf[par][head, :], axis=1, keepdims=True)
            m_prev = jnp.where(kj == 0, -jnp.inf, m_scr[head, :])
            m_new = jnp.maximum(m_prev, tile_max)
            alpha = jnp.exp2(m_prev - m_new)
            m_scr[head, :] = m_new
            p = jnp.exp2(s_buf[par][head, :] - m_new).astype(BF16)
            acc = acc_scr[head, :] * alpha
            acc = acc + jnp.dot(p, v, preferred_element_type=F32)
            acc_scr[head, :] = acc
            o = acc[:, 0:HEAD_DIM] / acc[:, HEAD_DIM:HEAD_DIM + 1]
            o_ref[rows, g * HEAD_DIM:(g + 1) * HEAD_DIM] = o.astype(BF16)

    m_scr[...] = jnp.zeros(m_scr.shape, F32)
    acc_scr[...] = jnp.zeros(acc_scr.shape, F32)
    stage_a(0, 0)

    def body(i, carry):
        for par in range(2):
            t = 2 * i + par
            stage_a(t + 1, 1 - par)
            stage_b(t, par)
        return carry

    lax.fori_loop(0, n_pairs // 2, body, 0)


def _attention(q, k, v, batch, seq):
    n = batch * seq
    tq = TQ_ATTN
    tk = min(TK_ATTN, seq)
    qs = min(QS_ATTN, seq)
    nq, nk, nsuper = qs // tq, seq // tk, seq // qs
    assert (nq * nk) % 2 == 0
    m = Q_PER_KV * tq
    kv_map = lambda b, h, si: (h, b, 0)
    return pl.pallas_call(
        functools.partial(_attn_kernel, tq=tq, tk=tk, nq=nq, nk=nk),
        grid=(batch, N_KV_HEADS, nsuper),
        in_specs=[
            pl.BlockSpec((Q_PER_KV, qs, HEAD_DIM), lambda b, h, si: (h, b * nsuper + si, 0)),
            pl.BlockSpec((1, seq, HEAD_DIM), kv_map),
            pl.BlockSpec((1, seq, 2 * HEAD_DIM), kv_map),
        ],
        out_specs=pl.BlockSpec((qs, Q_PER_KV * HEAD_DIM), lambda b, h, si: (b * nsuper + si, h)),
        out_shape=jax.ShapeDtypeStruct((n, ATTN_WIDTH), BF16),
        scratch_shapes=[
            pltpu.VMEM((m, tk), F32),
            pltpu.VMEM((m, tk), F32),
            pltpu.VMEM((m, 1), F32),
            pltpu.VMEM((m, 2 * HEAD_DIM), F32),
        ],
        compiler_params=pltpu.CompilerParams(
            dimension_semantics=("arbitrary",) * 3, vmem_limit_bytes=VMEM_LIMIT_BYTES),
        name="attention",
    )(q, k, v)


def _conv_kernel(prev_ref, main_ref, next_ref, w_ref, b_ref, lg_ref, lb_ref, o_ref,
                 ext_scr, sh_scr, y_scr, *, ts, tiles_per_seq):
    si = pl.program_id(0) % tiles_per_seq
    halo = HALO_ROWS
    lanes = HEAD_DIM
    nblk = CONV_WIDTH // lanes
    span = ts + 2 * halo - SUBLANES
    base = halo - CONV_PAD
    rows = ROWS_CONV
    groups = rows // SUBLANES
    has_prev = si > 0
    has_next = si < tiles_per_seq - 1

    for cb in range(nblk):
        cs = slice(cb * lanes, (cb + 1) * lanes)
        ext_scr[cb, 0:halo, :] = jnp.where(has_prev, prev_ref[:, cs].astype(F32), 0.0)
        ext_scr[cb, halo:halo + ts, :] = main_ref[:, cs].astype(F32)
        ext_scr[cb, halo + ts:halo + ts + halo, :] = jnp.where(has_next, next_ref[:, cs].astype(F32), 0.0)
        for res in range(1, SUBLANES):
            sh_scr[cb, res - 1] = ext_scr[cb, res:res + span, :]
        taps = [jnp.broadcast_to(w_ref[j:j + 1, cs], (SUBLANES, lanes)) for j in range(CONV_TAPS)]
        bias = jnp.broadcast_to(b_ref[:, cs], (SUBLANES, lanes))

        def row_block(rb, carry, cb=cb, taps=taps, bias=bias):
            r0 = pl.multiple_of(rb * rows, rows)
            accs = [bias] * groups
            for j in range(CONV_TAPS):
                off = base + j
                res, start = off % SUBLANES, r0 + (off // SUBLANES) * SUBLANES
                if res == 0:
                    win = ext_scr[cb, pl.ds(start, rows), :]
                else:
                    win = sh_scr[cb, res - 1, pl.ds(start, rows), :]
                for r in range(groups):
                    accs[r] = accs[r] + win[r * SUBLANES:(r + 1) * SUBLANES] * taps[j]
            y_scr[cb, pl.ds(r0, rows), :] = jnp.concatenate(accs, axis=0)
            return carry

        lax.fori_loop(0, ts // rows, row_block, 0)

    def plane(cb):
        return y_scr[cb, 0:ts, :]

    tot = plane(0)
    for cb in range(1, nblk):
        tot = tot + plane(cb)
    mu = jnp.sum(tot, axis=-1, keepdims=True) * (1.0 / CONV_WIDTH)
    sq = None
    for cb in range(nblk):
        d = plane(cb) - mu
        sq = d * d if sq is None else sq + d * d
    var = jnp.sum(sq, axis=-1, keepdims=True) * (1.0 / CONV_WIDTH)
    inv = lax.rsqrt(var + EPS)
    for cb in range(nblk):
        cs = slice(cb * lanes, (cb + 1) * lanes)
        z = (plane(cb) - mu) * inv * lg_ref[:, cs] + lb_ref[:, cs]
        o_ref[:, cs] = (z * _sigmoid(z)).astype(BF16)


def _conv_module(u, seq, dw_w, dw_b, ln_g, ln_b):
    n = u.shape[0]
    ts = TS_CONV
    tiles_per_seq = seq // ts
    hb = ts // HALO_ROWS
    last = n // HALO_ROWS - 1
    span = ts + 2 * HALO_ROWS - SUBLANES
    nblk = CONV_WIDTH // HEAD_DIM
    return pl.pallas_call(
        functools.partial(_conv_kernel, ts=ts, tiles_per_seq=tiles_per_seq),
        grid=(n // ts,),
        in_specs=[
            pl.BlockSpec((HALO_ROWS, CONV_WIDTH), lambda i: (jnp.maximum(i * hb - 1, 0), 0)),
            pl.BlockSpec((ts, CONV_WIDTH), lambda i: (i, 0)),
            pl.BlockSpec((HALO_ROWS, CONV_WIDTH), lambda i: (jnp.minimum((i + 1) * hb, last), 0)),
            _const_spec((CONV_TAPS, CONV_WIDTH)),
            _const_spec((1, CONV_WIDTH)),
            _const_spec((1, CONV_WIDTH)),
            _const_spec((1, CONV_WIDTH)),
        ],
        out_specs=pl.BlockSpec((ts, CONV_WIDTH), lambda i: (i, 0)),
        out_shape=jax.ShapeDtypeStruct((n, CONV_WIDTH), BF16),
        scratch_shapes=[
            pltpu.VMEM((nblk, ts + 2 * HALO_ROWS, HEAD_DIM), F32),
            pltpu.VMEM((nblk, SUBLANES - 1, span, HEAD_DIM), F32),
            pltpu.VMEM((nblk, ts + SUBLANES, HEAD_DIM), F32),
        ],
        compiler_params=pltpu.CompilerParams(
            dimension_semantics=("arbitrary",), vmem_limit_bytes=VMEM_LIMIT_BYTES),
        name="conv_module",
    )(u, u, u, dw_w, dw_b, ln_g, ln_b)


def _ffn_kernel(x_ref, a_ref, c_ref, wo_ref, g_ref, wg_ref, wu_ref, wd_ref, o_ref, h_scr):
    f = pl.program_id(1)

    @pl.when(f == 0)
    def _():
        mix = jnp.dot(a_ref[...], wo_ref[0:ATTN_WIDTH, :], preferred_element_type=F32)
        mix = mix + jnp.dot(c_ref[...], wo_ref[ATTN_WIDTH:, :], preferred_element_type=F32)
        x1 = x_ref[...] + mix
        o_ref[...] = x1
        ms = jnp.mean(x1 * x1, axis=-1, keepdims=True)
        h_scr[...] = (x1 * lax.rsqrt(ms + EPS) * g_ref[...]).astype(BF16)

    h = h_scr[...]
    g = jnp.dot(h, wg_ref[...], preferred_element_type=F32)
    u = jnp.dot(h, wu_ref[...], preferred_element_type=F32)
    act = (g * _sigmoid(g) * u).astype(BF16)
    o_ref[...] += jnp.dot(act, wd_ref[...], preferred_element_type=F32)


def _out_ffn(x2d, attn, conv, w_out, norm_g, w_gate, w_up, w_down):
    n = x2d.shape[0]
    tm, tf = TM_FFN, TF_FFN
    return pl.pallas_call(
        _ffn_kernel,
        grid=(n // tm, D_FF // tf),
        in_specs=[
            pl.BlockSpec((tm, D_MODEL), lambda i, f: (i, 0)),
            pl.BlockSpec((tm, ATTN_WIDTH), lambda i, f: (i, 0)),
            pl.BlockSpec((tm, CONV_WIDTH), lambda i, f: (i, 0)),
            _const_spec((D_MODEL, D_MODEL)),
            _const_spec((1, D_MODEL)),
            pl.BlockSpec((D_MODEL, tf), lambda i, f: (0, f)),
            pl.BlockSpec((D_MODEL, tf), lambda i, f: (0, f)),
            pl.BlockSpec((tf, D_MODEL), lambda i, f: (f, 0)),
        ],
        out_specs=pl.BlockSpec((tm, D_MODEL), lambda i, f: (i, 0)),
        out_shape=jax.ShapeDtypeStruct((n, D_MODEL), F32),
        scratch_shapes=[pltpu.VMEM((tm, D_MODEL), BF16)],
        compiler_params=pltpu.CompilerParams(
            dimension_semantics=("arbitrary", "arbitrary"), vmem_limit_bytes=VMEM_LIMIT_BYTES),
        name="out_ffn",
    )(x2d, attn, conv, w_out, norm_g, w_gate, w_up, w_down)


def _rope_tables(seq):
    rows = seq // GRID_W
    half = HEAD_DIM // 2
    inv_freq = ROPE_THETA ** (-jnp.arange(0, half, 2, dtype=F32) / half)
    ang_r = jnp.arange(rows).astype(F32)[:, None] * inv_freq[None, :]
    ang_c = jnp.arange(GRID_W).astype(F32)[:, None] * inv_freq[None, :]
    small = lax.optimization_barrier((jnp.cos(ang_r), jnp.sin(ang_r), jnp.cos(ang_c), jnp.sin(ang_c)))

    def expand(by_row, by_col):
        r = jnp.repeat(by_row, GRID_W, axis=0)
        c = jnp.tile(by_col, (rows, 1))
        return jnp.concatenate([r, r, c, c], axis=-1)

    cos, sin = expand(small[0], small[2]), expand(small[1], small[3])
    first = (jnp.arange(HEAD_DIM) // (HEAD_DIM // 4)) % 2 == 0
    sin_a = jnp.where(first[None, :], -sin, 0.0)
    sin_b = jnp.where(first[None, :], 0.0, sin)
    return (cos, sin_a, sin_b)


def _trunk(x, p):
    batch, seq, _ = x.shape
    x2d = x.reshape(batch * seq, D_MODEL)
    q, k, v, u = _in_projection(x2d, seq, p["norm_mix_g"], p["w_in"], p["b_glu"],
                                p["q_norm_g"], p["k_norm_g"], _rope_tables(seq))
    attn = _attention(q, k, v, batch, seq)
    conv = _conv_module(u, seq, p["dw_w"], p["dw_b"], p["conv_ln_g"], p["conv_ln_b"])
    y = _out_ffn(x2d, attn, conv, p["w_out"], p["norm_ffn_g"], p["w_gate"], p["w_up"], p["w_down"])
    return y.reshape(batch, seq, D_MODEL)


def _prepare(norm_mix_g, w_in, b_glu, q_norm_g, k_norm_g, dw_w, dw_b,
             conv_ln_g, conv_ln_b, w_out, norm_ffn_g, w_gate, w_up, w_down):
    assert norm_mix_g.shape[0] == 1, "single-layer trunk"
    return {
        "norm_mix_g": norm_mix_g[0][None, :],
        "w_in": w_in[0].astype(BF16),
        "b_glu": b_glu[0][None, :],
        "q_norm_g": q_norm_g[0][None, :],
        "k_norm_g": k_norm_g[0][None, :],
        "dw_w": dw_w[0],
        "dw_b": dw_b[0][None, :],
        "conv_ln_g": conv_ln_g[0][None, :],
        "conv_ln_b": conv_ln_b[0][None, :],
        "w_out": w_out[0].astype(BF16),
        "norm_ffn_g": norm_ffn_g[0][None, :],
        "w_gate": w_gate[0].astype(BF16),
        "w_up": w_up[0].astype(BF16),
        "w_down": w_down[0].astype(BF16),
    }


def kernel(x_prompt, x_sample, norm_mix_g, w_in, b_glu, q_norm_g, k_norm_g, dw_w, dw_b,
           conv_ln_g, conv_ln_b, w_out, norm_ffn_g, w_gate, w_up, w_down):
    p = _prepare(norm_mix_g, w_in, b_glu, q_norm_g, k_norm_g, dw_w, dw_b,
                 conv_ln_g, conv_ln_b, w_out, norm_ffn_g, w_gate, w_up, w_down)
    return (_trunk(x_prompt, p), _trunk(x_sample, p))
```

```python
import functools

import jax
import jax.numpy as jnp
from jax import lax
from jax.experimental import pallas as pl
from jax.experimental.pallas import tpu as pltpu

D_MODEL = 2048
GRID_W = 64
HEAD_DIM = 128
N_Q_HEADS = 8
N_KV_HEADS = 2
Q_PER_KV = N_Q_HEADS // N_KV_HEADS
ATTN_WIDTH = N_Q_HEADS * HEAD_DIM
KV_WIDTH = N_KV_HEADS * HEAD_DIM
CONV_WIDTH = D_MODEL - ATTN_WIDTH
CONV_TAPS = 31
CONV_PAD = CONV_TAPS // 2
IN_COLS = ATTN_WIDTH + 2 * KV_WIDTH + 2 * CONV_WIDTH
D_FF = 5632
ROPE_THETA = 10000.0
EPS = 1e-6
LOG2_E = 1.4426950408889634

F32 = jnp.float32
BF16 = jnp.bfloat16

V7X_VMEM_BYTES = 64 * 1024 * 1024
VMEM_LIMIT_BYTES = V7X_VMEM_BYTES - 8 * 1024 * 1024
SUBLANES = 8
BF16_ROWS = 16
HALO_ROWS = BF16_ROWS
assert HALO_ROWS >= CONV_PAD

TM_PROJ = 512
TQ_ATTN = 256
TK_ATTN = 2048
QS_ATTN = 4096
TS_CONV = 512
ROWS_CONV = 128
TM_FFN = 512
TF_FFN = 512


def _const_spec(shape):
    return pl.BlockSpec(shape, lambda *_: (0,) * len(shape), pipeline_mode=pl.Buffered(1))


def _sigmoid(x):
    return 1.0 / (1.0 + jnp.exp(-x))


def _inproj_kernel(x_ref, g_ref, w_ref, b_ref, qg_ref, kg_ref, cos_ref, sa_ref, sb_ref,
                   q_ref, k_ref, v_ref, u_ref, h_scr):
    x = x_ref[...]
    ms = jnp.mean(x * x, axis=-1, keepdims=True)
    h_scr[...] = (x * lax.rsqrt(ms + EPS) * g_ref[...]).astype(BF16)
    h = h_scr[...]

    def norm_rope(z, gain, c, sa, sb):
        z = z * lax.rsqrt(jnp.mean(z * z, axis=-1, keepdims=True) + EPS) * gain
        fwd = pltpu.roll(z, HEAD_DIM - HEAD_DIM // 4, 1)
        bwd = pltpu.roll(z, HEAD_DIM // 4, 1)
        return z * c + fwd * sa + bwd * sb

    cos, sa, sb = cos_ref[...], sa_ref[...], sb_ref[...]
    qg = qg_ref[...] * (HEAD_DIM ** -0.5 * LOG2_E)
    kg = kg_ref[...]

    half = ATTN_WIDTH // 2
    for c in range(2):
        z = jnp.dot(h, w_ref[:, c * half:(c + 1) * half], preferred_element_type=F32)
        for i in range(half // HEAD_DIM):
            zi = z[:, i * HEAD_DIM:(i + 1) * HEAD_DIM]
            q_ref[c * (half // HEAD_DIM) + i] = norm_rope(zi, qg, cos, sa, sb).astype(BF16)

    z = jnp.dot(h, w_ref[:, ATTN_WIDTH:ATTN_WIDTH + 2 * KV_WIDTH], preferred_element_type=F32)
    for i in range(N_KV_HEADS):
        zi = z[:, i * HEAD_DIM:(i + 1) * HEAD_DIM]
        k_ref[i] = norm_rope(zi, kg, cos, sa, sb).astype(BF16)
        v_ref[i, :, 0:HEAD_DIM] = z[:, KV_WIDTH + i * HEAD_DIM:KV_WIDTH + (i + 1) * HEAD_DIM].astype(BF16)
        v_ref[i, :, HEAD_DIM:] = jnp.ones((x.shape[0], HEAD_DIM), BF16)

    u0 = ATTN_WIDTH + 2 * KV_WIDTH
    cw = CONV_WIDTH // 2
    for c in range(2):
        a = jnp.dot(h, w_ref[:, u0 + c * cw:u0 + (c + 1) * cw], preferred_element_type=F32)
        a = a + b_ref[:, c * cw:(c + 1) * cw]
        gt = jnp.dot(h, w_ref[:, u0 + CONV_WIDTH + c * cw:u0 + CONV_WIDTH + (c + 1) * cw],
                     preferred_element_type=F32)
        gt = gt + b_ref[:, CONV_WIDTH + c * cw:CONV_WIDTH + (c + 1) * cw]
        u_ref[:, c * cw:(c + 1) * cw] = (a * _sigmoid(gt)).astype(BF16)


def _in_projection(x2d, seq, norm_g, w_in, b_glu, q_g, k_g, tables):
    n = x2d.shape[0]
    tm = TM_PROJ
    tiles_per_seq = seq // tm
    row = lambda i: (i, 0)
    tab = lambda i: (i % tiles_per_seq, 0)
    head_major = lambda i: (0, i, 0)
    tab_spec = pl.BlockSpec((tm, HEAD_DIM), tab)
    return pl.pallas_call(
        _inproj_kernel,
        grid=(n // tm,),
        in_specs=[
            pl.BlockSpec((tm, D_MODEL), row),
            _const_spec((1, D_MODEL)),
            _const_spec((D_MODEL, IN_COLS)),
            _const_spec((1, 2 * CONV_WIDTH)),
            _const_spec((1, HEAD_DIM)),
            _const_spec((1, HEAD_DIM)),
            tab_spec, tab_spec, tab_spec,
        ],
        out_specs=[
            pl.BlockSpec((N_Q_HEADS, tm, HEAD_DIM), head_major),
            pl.BlockSpec((N_KV_HEADS, tm, HEAD_DIM), head_major),
            pl.BlockSpec((N_KV_HEADS, tm, 2 * HEAD_DIM), head_major),
            pl.BlockSpec((tm, CONV_WIDTH), row),
        ],
        out_shape=[
            jax.ShapeDtypeStruct((N_Q_HEADS, n, HEAD_DIM), BF16),
            jax.ShapeDtypeStruct((N_KV_HEADS, n, HEAD_DIM), BF16),
            jax.ShapeDtypeStruct((N_KV_HEADS, n, 2 * HEAD_DIM), BF16),
            jax.ShapeDtypeStruct((n, CONV_WIDTH), BF16),
        ],
        scratch_shapes=[pltpu.VMEM((tm, D_MODEL), BF16)],
        compiler_params=pltpu.CompilerParams(
            dimension_semantics=("arbitrary",), vmem_limit_bytes=VMEM_LIMIT_BYTES),
        name="in_projection",
    )(x2d, norm_g, w_in, b_glu, q_g, k_g, *tables)


def _attn_kernel(q_ref, k_ref, v_ref, o_ref, s0, s1, m_scr, acc_scr, *, tq, tk, nq, nk):
    m_rows = Q_PER_KV * tq
    n_pairs = nq * nk
    s_buf = (s0, s1)

    def split(t):
        t = jnp.minimum(t, n_pairs - 1)
        return t // nk, t % nk

    def stage_a(t, par):
        qi, kj = split(t)
        q = q_ref[:, pl.ds(pl.multiple_of(qi * tq, tq), tq), :].reshape(m_rows, HEAD_DIM)
        k = k_ref[0, pl.ds(pl.multiple_of(kj * tk, tk), tk), :]
        s_buf[par][...] = lax.dot_general(q, k, (((1,), (1,)), ((), ())),
                                          preferred_element_type=F32)

    def stage_b(t, par):
        qi, kj = split(t)
        v = v_ref[0, pl.ds(pl.multiple_of(kj * tk, tk), tk), :]
        rows = pl.ds(pl.multiple_of(qi * tq, tq), tq)
        for g in range(Q_PER_KV):
            head = slice(g * tq, (g + 1) * tq)
            tile_max = jnp.max(s_buf[par][head, :], axis=1, keepdims=True)
            m_prev = jnp.where(kj == 0, -jnp.inf, m_scr[head, :])
            m_new = jnp.maximum(m_prev, tile_max)
            alpha = jnp.exp2(m_prev - m_new)
            m_scr[head, :] = m_new
            p = jnp.exp2(s_buf[par][head, :] - m_new).astype(BF16)
            acc = acc_scr[head, :] * alpha
            acc = acc + jnp.dot(p, v, preferred_element_type=F32)
            acc_scr[head, :] = acc
            o = acc[:, 0:HEAD_DIM] / acc[:, HEAD_DIM:HEAD_DIM + 1]
            o_ref[rows, g * HEAD_DIM:(g + 1) * HEAD_DIM] = o.astype(BF16)

    m_scr[...] = jnp.zeros(m_scr.shape, F32)
    acc_scr[...] = jnp.zeros(acc_scr.shape, F32)
    stage_a(0, 0)

    def body(i, carry):
        for par in range(2):
            t = 2 * i + par
            stage_a(t + 1, 1 - par)
            stage_b(t, par)
        return carry

    lax.fori_loop(0, n_pairs // 2, body, 0)


def _attention(q, k, v, batch, seq):
    n = batch * seq
    tq = TQ_ATTN
    tk = min(TK_ATTN, seq)
    qs = min(QS_ATTN, seq)
    nq, nk, nsuper = qs // tq, seq // tk, seq // qs
    assert (nq * nk) % 2 == 0
    m = Q_PER_KV * tq
    kv_map = lambda b, h, si: (h, b, 0)
    return pl.pallas_call(
        functools.partial(_attn_kernel, tq=tq, tk=tk, nq=nq, nk=nk),
        grid=(batch, N_KV_HEADS, nsuper),
        in_specs=[
            pl.BlockSpec((Q_PER_KV, qs, HEAD_DIM), lambda b, h, si: (h, b * nsuper + si, 0)),
            pl.BlockSpec((1, seq, HEAD_DIM), kv_map),
            pl.BlockSpec((1, seq, 2 * HEAD_DIM), kv_map),
        ],
        out_specs=pl.BlockSpec((qs, Q_PER_KV * HEAD_DIM), lambda b, h, si: (b * nsuper + si, h)),
        out_shape=jax.ShapeDtypeStruct((n, ATTN_WIDTH), BF16),
        scratch_shapes=[
            pltpu.VMEM((m, tk), F32),
            pltpu.VMEM((m, tk), F32),
            pltpu.VMEM((m, 1), F32),
            pltpu.VMEM((m, 2 * HEAD_DIM), F32),
        ],
        compiler_params=pltpu.CompilerParams(
            dimension_semantics=("arbitrary",) * 3, vmem_limit_bytes=VMEM_LIMIT_BYTES),
        name="attention",
    )(q, k, v)


def _conv_kernel(prev_ref, main_ref, next_ref, w_ref, b_ref, lg_ref, lb_ref, o_ref,
                 ext_scr, sh_scr, y_scr, *, ts, tiles_per_seq):
    si = pl.program_id(0) % tiles_per_seq
    halo = HALO_ROWS
    lanes = HEAD_DIM
    nblk = CONV_WIDTH // lanes
    span = ts + 2 * halo - SUBLANES
    base = halo - CONV_PAD
    rows = ROWS_CONV
    groups = rows // SUBLANES
    has_prev = si > 0
    has_next = si < tiles_per_seq - 1

    for cb in range(nblk):
        cs = slice(cb * lanes, (cb + 1) * lanes)
        ext_scr[cb, 0:halo, :] = jnp.where(has_prev, prev_ref[:, cs].astype(F32), 0.0)
        ext_scr[cb, halo:halo + ts, :] = main_ref[:, cs].astype(F32)
        ext_scr[cb, halo + ts:halo + ts + halo, :] = jnp.where(has_next, next_ref[:, cs].astype(F32), 0.0)
        for res in range(1, SUBLANES):
            sh_scr[cb, res - 1] = ext_scr[cb, res:res + span, :]
        taps = [jnp.broadcast_to(w_ref[j:j + 1, cs], (SUBLANES, lanes)) for j in range(CONV_TAPS)]
        bias = jnp.broadcast_to(b_ref[:, cs], (SUBLANES, lanes))

        def row_block(rb, carry, cb=cb, taps=taps, bias=bias):
            r0 = pl.multiple_of(rb * rows, rows)
            accs = [bias] * groups
            for j in range(CONV_TAPS):
                off = base + j
                res, start = off % SUBLANES, r0 + (off // SUBLANES) * SUBLANES
                if res == 0:
                    win = ext_scr[cb, pl.ds(start, rows), :]
                else:
                    win = sh_scr[cb, res - 1, pl.ds(start, rows), :]
                for r in range(groups):
                    accs[r] = accs[r] + win[r * SUBLANES:(r + 1) * SUBLANES] * taps[j]
            y_scr[cb, pl.ds(r0, rows), :] = jnp.concatenate(accs, axis=0)
            return carry

        lax.fori_loop(0, ts // rows, row_block, 0)

    def plane(cb):
        return y_scr[cb, 0:ts, :]

    tot = plane(0)
    for cb in range(1, nblk):
        tot = tot + plane(cb)
    mu = jnp.sum(tot, axis=-1, keepdims=True) * (1.0 / CONV_WIDTH)
    sq = None
    for cb in range(nblk):
        d = plane(cb) - mu
        sq = d * d if sq is None else sq + d * d
    var = jnp.sum(sq, axis=-1, keepdims=True) * (1.0 / CONV_WIDTH)
    inv = lax.rsqrt(var + EPS)
    for cb in range(nblk):
        cs = slice(cb * lanes, (cb + 1) * lanes)
        z = (plane(cb) - mu) * inv * lg_ref[:, cs] + lb_ref[:, cs]
        o_ref[:, cs] = (z * _sigmoid(z)).astype(BF16)


def _conv_module(u, seq, dw_w, dw_b, ln_g, ln_b):
    n = u.shape[0]
    ts = TS_CONV
    tiles_per_seq = seq // ts
    hb = ts // HALO_ROWS
    last = n // HALO_ROWS - 1
    span = ts + 2 * HALO_ROWS - SUBLANES
    nblk = CONV_WIDTH // HEAD_DIM
    return pl.pallas_call(
        functools.partial(_conv_kernel, ts=ts, tiles_per_seq=tiles_per_seq),
        grid=(n // ts,),
        in_specs=[
            pl.BlockSpec((HALO_ROWS, CONV_WIDTH), lambda i: (jnp.maximum(i * hb - 1, 0), 0)),
            pl.BlockSpec((ts, CONV_WIDTH), lambda i: (i, 0)),
            pl.BlockSpec((HALO_ROWS, CONV_WIDTH), lambda i: (jnp.minimum((i + 1) * hb, last), 0)),
            _const_spec((CONV_TAPS, CONV_WIDTH)),
            _const_spec((1, CONV_WIDTH)),
            _const_spec((1, CONV_WIDTH)),
            _const_spec((1, CONV_WIDTH)),
        ],
        out_specs=pl.BlockSpec((ts, CONV_WIDTH), lambda i: (i, 0)),
        out_shape=jax.ShapeDtypeStruct((n, CONV_WIDTH), BF16),
        scratch_shapes=[
            pltpu.VMEM((nblk, ts + 2 * HALO_ROWS, HEAD_DIM), F32),
            pltpu.VMEM((nblk, SUBLANES - 1, span, HEAD_DIM), F32),
            pltpu.VMEM((nblk, ts + SUBLANES, HEAD_DIM), F32),
        ],
        compiler_params=pltpu.CompilerParams(
            dimension_semantics=("arbitrary",), vmem_limit_bytes=VMEM_LIMIT_BYTES),
        name="conv_module",
    )(u, u, u, dw_w, dw_b, ln_g, ln_b)


def _ffn_kernel(x_ref, a_ref, c_ref, wo_ref, g_ref, wg_ref, wu_ref, wd_ref, o_ref, h_scr):
    f = pl.program_id(1)

    @pl.when(f == 0)
    def _():
        mix = jnp.dot(a_ref[...], wo_ref[0:ATTN_WIDTH, :], preferred_element_type=F32)
        mix = mix + jnp.dot(c_ref[...], wo_ref[ATTN_WIDTH:, :], preferred_element_type=F32)
        x1 = x_ref[...] + mix
        o_ref[...] = x1
        ms = jnp.mean(x1 * x1, axis=-1, keepdims=True)
        h_scr[...] = (x1 * lax.rsqrt(ms + EPS) * g_ref[...]).astype(BF16)

    h = h_scr[...]
    g = jnp.dot(h, wg_ref[...], preferred_element_type=F32)
    u = jnp.dot(h, wu_ref[...], preferred_element_type=F32)
    act = (g * _sigmoid(g) * u).astype(BF16)
    o_ref[...] += jnp.dot(act, wd_ref[...], preferred_element_type=F32)


def _out_ffn(x2d, attn, conv, w_out, norm_g, w_gate, w_up, w_down):
    n = x2d.shape[0]
    tm, tf = TM_FFN, TF_FFN
    return pl.pallas_call(
        _ffn_kernel,
        grid=(n // tm, D_FF // tf),
        in_specs=[
            pl.BlockSpec((tm, D_MODEL), lambda i, f: (i, 0)),
            pl.BlockSpec((tm, ATTN_WIDTH), lambda i, f: (i, 0)),
            pl.BlockSpec((tm, CONV_WIDTH), lambda i, f: (i, 0)),
            _const_spec((D_MODEL, D_MODEL)),
            _const_spec((1, D_MODEL)),
            pl.BlockSpec((D_MODEL, tf), lambda i, f: (0, f)),
            pl.BlockSpec((D_MODEL, tf), lambda i, f: (0, f)),
            pl.BlockSpec((tf, D_MODEL), lambda i, f: (f, 0)),
        ],
        out_specs=pl.BlockSpec((tm, D_MODEL), lambda i, f: (i, 0)),
        out_shape=jax.ShapeDtypeStruct((n, D_MODEL), F32),
        scratch_shapes=[pltpu.VMEM((tm, D_MODEL), BF16)],
        compiler_params=pltpu.CompilerParams(
            dimension_semantics=("arbitrary", "arbitrary"), vmem_limit_bytes=VMEM_LIMIT_BYTES),
        name="out_ffn",
    )(x2d, attn, conv, w_out, norm_g, w_gate, w_up, w_down)


def _rope_tables(seq):
    rows = seq // GRID_W
    half = HEAD_DIM // 2
    inv_freq = ROPE_THETA ** (-jnp.arange(0, half, 2, dtype=F32) / half)
    ang_r = jnp.arange(rows).astype(F32)[:, None] * inv_freq[None, :]
    ang_c = jnp.arange(GRID_W).astype(F32)[:, None] * inv_freq[None, :]
    small = lax.optimization_barrier((jnp.cos(ang_r), jnp.sin(ang_r), jnp.cos(ang_c), jnp.sin(ang_c)))

    def expand(by_row, by_col):
        r = jnp.repeat(by_row, GRID_W, axis=0)
        c = jnp.tile(by_col, (rows, 1))
        return jnp.concatenate([r, r, c, c], axis=-1)

    cos, sin = expand(small[0], small[2]), expand(small[1], small[3])
    first = (jnp.arange(HEAD_DIM) // (HEAD_DIM // 4)) % 2 == 0
    sin_a = jnp.where(first[None, :], -sin, 0.0)
    sin_b = jnp.where(first[None, :], 0.0, sin)
    return (cos, sin_a, sin_b)


def _trunk(x, p):
    batch, seq, d_model = x.shape
    assert d_model == D_MODEL and x.dtype == F32
    assert seq % GRID_W == 0 and seq % TM_PROJ == 0 and seq % TS_CONV == 0
    assert seq % min(TK_ATTN, seq) == 0 and seq % min(QS_ATTN, seq) == 0 and min(QS_ATTN, seq) % TQ_ATTN == 0
    assert (batch * seq) % TM_FFN == 0 and TS_CONV % ROWS_CONV == 0 and D_FF % TF_FFN == 0
    x2d = x.reshape(batch * seq, D_MODEL)
    q, k, v, u = _in_projection(x2d, seq, p["norm_mix_g"], p["w_in"], p["b_glu"],
                                p["q_norm_g"], p["k_norm_g"], _rope_tables(seq))
    attn = _attention(q, k, v, batch, seq)
    conv = _conv_module(u, seq, p["dw_w"], p["dw_b"], p["conv_ln_g"], p["conv_ln_b"])
    y = _out_ffn(x2d, attn, conv, p["w_out"], p["norm_ffn_g"], p["w_gate"], p["w_up"], p["w_down"])
    return y.reshape(batch, seq, D_MODEL)


def _prepare(norm_mix_g, w_in, b_glu, q_norm_g, k_norm_g, dw_w, dw_b,
             conv_ln_g, conv_ln_b, w_out, norm_ffn_g, w_gate, w_up, w_down):
    assert norm_mix_g.shape[0] == 1, "single-layer trunk"
    return {
        "norm_mix_g": norm_mix_g[0][None, :],
        "w_in": w_in[0].astype(BF16),
        "b_glu": b_glu[0][None, :],
        "q_norm_g": q_norm_g[0][None, :],
        "k_norm_g": k_norm_g[0][None, :],
        "dw_w": dw_w[0],
        "dw_b": dw_b[0][None, :],
        "conv_ln_g": conv_ln_g[0][None, :],
        "conv_ln_b": conv_ln_b[0][None, :],
        "w_out": w_out[0].astype(BF16),
        "norm_ffn_g": norm_ffn_g[0][None, :],
        "w_gate": w_gate[0].astype(BF16),
        "w_up": w_up[0].astype(BF16),
        "w_down": w_down[0].astype(BF16),
    }


def kernel(x_prompt, x_sample, norm_mix_g, w_in, b_glu, q_norm_g, k_norm_g, dw_w, dw_b,
           conv_ln_g, conv_ln_b, w_out, norm_ffn_g, w_gate, w_up, w_down):
    p = _prepare(norm_mix_g, w_in, b_glu, q_norm_g, k_norm_g, dw_w, dw_b,
                 conv_ln_g, conv_ln_b, w_out, norm_ffn_g, w_gate, w_up, w_down)
    return (_trunk(x_prompt, p), _trunk(x_sample, p))
```

```python
import functools

import jax
import jax.numpy as jnp
from jax import lax
from jax.experimental import pallas as pl
from jax.experimental.pallas import tpu as pltpu

D_MODEL = 2048
GRID_W = 64
HEAD_DIM = 128
N_Q_HEADS = 8
N_KV_HEADS = 2
Q_PER_KV = N_Q_HEADS // N_KV_HEADS
ATTN_WIDTH = N_Q_HEADS * HEAD_DIM
KV_WIDTH = N_KV_HEADS * HEAD_DIM
CONV_WIDTH = D_MODEL - ATTN_WIDTH
CONV_TAPS = 31
CONV_PAD = CONV_TAPS // 2
IN_COLS = ATTN_WIDTH + 2 * KV_WIDTH + 2 * CONV_WIDTH
D_FF = 5632
ROPE_THETA = 10000.0
EPS = 1e-6
LOG2_E = 1.4426950408889634

F32 = jnp.float32
BF16 = jnp.bfloat16

V7X_VMEM_BYTES = 64 * 1024 * 1024
VMEM_LIMIT_BYTES = V7X_VMEM_BYTES - 8 * 1024 * 1024
SUBLANES = 8
BF16_ROWS = 16
HALO_ROWS = BF16_ROWS
assert HALO_ROWS >= CONV_PAD

TM_PROJ = 512
TQ_ATTN = 256
TK_ATTN = 2048
QS_ATTN = 4096
TS_CONV = 512
ROWS_CONV = 128
TM_FFN = 512
TF_FFN = 512


def _const_spec(shape):
    return pl.BlockSpec(shape, lambda *_: (0,) * len(shape), pipeline_mode=pl.Buffered(1))


def _sigmoid(x):
    return 1.0 / (1.0 + jnp.exp(-x))


def _inproj_kernel(x_ref, g_ref, w_ref, b_ref, qg_ref, kg_ref, cos_ref, sa_ref, sb_ref,
                   q_ref, k_ref, v_ref, u_ref, h_scr):
    x = x_ref[...]
    ms = jnp.mean(x * x, axis=-1, keepdims=True)
    h_scr[...] = (x * lax.rsqrt(ms + EPS) * g_ref[...]).astype(BF16)
    h = h_scr[...]

    def norm_rope(z, gain, c, sa, sb):
        z = z * lax.rsqrt(jnp.mean(z * z, axis=-1, keepdims=True) + EPS) * gain
        fwd = pltpu.roll(z, HEAD_DIM - HEAD_DIM // 4, 1)
        bwd = pltpu.roll(z, HEAD_DIM // 4, 1)
        return z * c + fwd * sa + bwd * sb

    cos, sa, sb = cos_ref[...], sa_ref[...], sb_ref[...]
    qg = qg_ref[...] * (HEAD_DIM ** -0.5 * LOG2_E)
    kg = kg_ref[...]

    half = ATTN_WIDTH // 2
    for c in range(2):
        z = jnp.dot(h, w_ref[:, c * half:(c + 1) * half], preferred_element_type=F32)
        for i in range(half // HEAD_DIM):
            zi = z[:, i * HEAD_DIM:(i + 1) * HEAD_DIM]
            q_ref[c * (half // HEAD_DIM) + i] = norm_rope(zi, qg, cos, sa, sb).astype(BF16)

    z = jnp.dot(h, w_ref[:, ATTN_WIDTH:ATTN_WIDTH + 2 * KV_WIDTH], preferred_element_type=F32)
    for i in range(N_KV_HEADS):
        zi = z[:, i * HEAD_DIM:(i + 1) * HEAD_DIM]
        k_ref[i] = norm_rope(zi, kg, cos, sa, sb).T.astype(BF16)
        v_ref[i, :, 0:HEAD_DIM] = z[:, KV_WIDTH + i * HEAD_DIM:KV_WIDTH + (i + 1) * HEAD_DIM].astype(BF16)
        v_ref[i, :, HEAD_DIM:] = jnp.ones((x.shape[0], HEAD_DIM), BF16)

    u0 = ATTN_WIDTH + 2 * KV_WIDTH
    cw = CONV_WIDTH // 2
    for c in range(2):
        a = jnp.dot(h, w_ref[:, u0 + c * cw:u0 + (c + 1) * cw], preferred_element_type=F32)
        a = a + b_ref[:, c * cw:(c + 1) * cw]
        gt = jnp.dot(h, w_ref[:, u0 + CONV_WIDTH + c * cw:u0 + CONV_WIDTH + (c + 1) * cw],
                     preferred_element_type=F32)
        gt = gt + b_ref[:, CONV_WIDTH + c * cw:CONV_WIDTH + (c + 1) * cw]
        u_ref[:, c * cw:(c + 1) * cw] = (a * _sigmoid(gt)).astype(BF16)


def _in_projection(x2d, seq, norm_g, w_in, b_glu, q_g, k_g, tables):
    n = x2d.shape[0]
    tm = TM_PROJ
    tiles_per_seq = seq // tm
    row = lambda i: (i, 0)
    tab = lambda i: (i % tiles_per_seq, 0)
    head_major = lambda i: (0, i, 0)
    tab_spec = pl.BlockSpec((tm, HEAD_DIM), tab)
    return pl.pallas_call(
        _inproj_kernel,
        grid=(n // tm,),
        in_specs=[
            pl.BlockSpec((tm, D_MODEL), row),
            _const_spec((1, D_MODEL)),
            _const_spec((D_MODEL, IN_COLS)),
            _const_spec((1, 2 * CONV_WIDTH)),
            _const_spec((1, HEAD_DIM)),
            _const_spec((1, HEAD_DIM)),
            tab_spec, tab_spec, tab_spec,
        ],
        out_specs=[
            pl.BlockSpec((N_Q_HEADS, tm, HEAD_DIM), head_major),
            pl.BlockSpec((N_KV_HEADS, HEAD_DIM, tm), lambda i: (0, 0, i)),
            pl.BlockSpec((N_KV_HEADS, tm, 2 * HEAD_DIM), head_major),
            pl.BlockSpec((tm, CONV_WIDTH), row),
        ],
        out_shape=[
            jax.ShapeDtypeStruct((N_Q_HEADS, n, HEAD_DIM), BF16),
            jax.ShapeDtypeStruct((N_KV_HEADS, HEAD_DIM, n), BF16),
            jax.ShapeDtypeStruct((N_KV_HEADS, n, 2 * HEAD_DIM), BF16),
            jax.ShapeDtypeStruct((n, CONV_WIDTH), BF16),
        ],
        scratch_shapes=[pltpu.VMEM((tm, D_MODEL), BF16)],
        compiler_params=pltpu.CompilerParams(
            dimension_semantics=("arbitrary",), vmem_limit_bytes=VMEM_LIMIT_BYTES),
        name="in_projection",
    )(x2d, norm_g, w_in, b_glu, q_g, k_g, *tables)


def _attn_kernel(q_ref, k_ref, v_ref, o_ref, s0, s1, m_scr, acc_scr, *, tq, tk, nq, nk):
    m_rows = Q_PER_KV * tq
    n_pairs = nq * nk
    s_buf = (s0, s1)

    def split(t):
        t = jnp.minimum(t, n_pairs - 1)
        return t // nk, t % nk

    def stage_a(t, par):
        qi, kj = split(t)
        q = q_ref[:, pl.ds(pl.multiple_of(qi * tq, tq), tq), :].reshape(m_rows, HEAD_DIM)
        k_t = k_ref[0, :, pl.ds(pl.multiple_of(kj * tk, tk), tk)]
        s_buf[par][...] = jnp.dot(q, k_t, preferred_element_type=F32)

    def stage_b(t, par):
        qi, kj = split(t)
        v = v_ref[0, pl.ds(pl.multiple_of(kj * tk, tk), tk), :]
        rows = pl.ds(pl.multiple_of(qi * tq, tq), tq)
        for g in range(Q_PER_KV):
            head = slice(g * tq, (g + 1) * tq)
            tile_max = jnp.max(s_buf[par][head, :], axis=1, keepdims=True)
            m_prev = jnp.where(kj == 0, -jnp.inf, m_scr[head, :])
            m_new = jnp.maximum(m_prev, tile_max)
            alpha = jnp.exp2(m_prev - m_new)
            m_scr[head, :] = m_new
            p = jnp.exp2(s_buf[par][head, :] - m_new).astype(BF16)
            acc = acc_scr[head, :] * alpha
            acc = acc + jnp.dot(p, v, preferred_element_type=F32)
            acc_scr[head, :] = acc
            o = acc[:, 0:HEAD_DIM] / acc[:, HEAD_DIM:HEAD_DIM + 1]
            o_ref[rows, g * HEAD_DIM:(g + 1) * HEAD_DIM] = o.astype(BF16)

    m_scr[...] = jnp.zeros(m_scr.shape, F32)
    acc_scr[...] = jnp.zeros(acc_scr.shape, F32)
    stage_a(0, 0)

    def body(i, carry):
        for par in range(2):
            t = 2 * i + par
            stage_a(t + 1, 1 - par)
            stage_b(t, par)
        return carry

    lax.fori_loop(0, n_pairs // 2, body, 0)


def _attention(q, k, v, batch, seq):
    n = batch * seq
    tq = TQ_ATTN
    tk = min(TK_ATTN, seq)
    qs = min(QS_ATTN, seq)
    nq, nk, nsuper = qs // tq, seq // tk, seq // qs
    assert (nq * nk) % 2 == 0
    m = Q_PER_KV * tq
    kv_map = lambda b, h, si: (h, b, 0)
    return pl.pallas_call(
        functools.partial(_attn_kernel, tq=tq, tk=tk, nq=nq, nk=nk),
        grid=(batch, N_KV_HEADS, nsuper),
        in_specs=[
            pl.BlockSpec((Q_PER_KV, qs, HEAD_DIM), lambda b, h, si: (h, b * nsuper + si, 0)),
            pl.BlockSpec((1, HEAD_DIM, seq), lambda b, h, si: (h, 0, b)),
            pl.BlockSpec((1, seq, 2 * HEAD_DIM), kv_map),
        ],
        out_specs=pl.BlockSpec((qs, Q_PER_KV * HEAD_DIM), lambda b, h, si: (b * nsuper + si, h)),
        out_shape=jax.ShapeDtypeStruct((n, ATTN_WIDTH), BF16),
        scratch_shapes=[
            pltpu.VMEM((m, tk), F32),
            pltpu.VMEM((m, tk), F32),
            pltpu.VMEM((m, 1), F32),
            pltpu.VMEM((m, 2 * HEAD_DIM), F32),
        ],
        compiler_params=pltpu.CompilerParams(
            dimension_semantics=("arbitrary",) * 3, vmem_limit_bytes=VMEM_LIMIT_BYTES),
        name="attention",
    )(q, k, v)


def _conv_kernel(prev_ref, main_ref, next_ref, w_ref, b_ref, lg_ref, lb_ref, o_ref,
                 ext_scr, sh_scr, y_scr, *, ts, tiles_per_seq):
    si = pl.program_id(0) % tiles_per_seq
    halo = HALO_ROWS
    lanes = HEAD_DIM
    nblk = CONV_WIDTH // lanes
    span = ts + 2 * halo - SUBLANES
    base = halo - CONV_PAD
    rows = ROWS_CONV
    groups = rows // SUBLANES
    has_prev = si > 0
    has_next = si < tiles_per_seq - 1

    for cb in range(nblk):
        cs = slice(cb * lanes, (cb + 1) * lanes)
        ext_scr[cb, 0:halo, :] = jnp.where(has_prev, prev_ref[:, cs].astype(F32), 0.0)
        ext_scr[cb, halo:halo + ts, :] = main_ref[:, cs].astype(F32)
        ext_scr[cb, halo + ts:halo + ts + halo, :] = jnp.where(has_next, next_ref[:, cs].astype(F32), 0.0)
        for res in range(1, SUBLANES):
            sh_scr[cb, res - 1] = ext_scr[cb, res:res + span, :]
        taps = [jnp.broadcast_to(w_ref[j:j + 1, cs], (SUBLANES, lanes)) for j in range(CONV_TAPS)]
        bias = jnp.broadcast_to(b_ref[:, cs], (SUBLANES, lanes))

        def row_block(rb, carry, cb=cb, taps=taps, bias=bias):
            r0 = pl.multiple_of(rb * rows, rows)
            accs = [bias] * groups
            for j in range(CONV_TAPS):
                off = base + j
                res, start = off % SUBLANES, r0 + (off // SUBLANES) * SUBLANES
                if res == 0:
                    win = ext_scr[cb, pl.ds(start, rows), :]
                else:
                    win = sh_scr[cb, res - 1, pl.ds(start, rows), :]
                for r in range(groups):
                    accs[r] = accs[r] + win[r * SUBLANES:(r + 1) * SUBLANES] * taps[j]
            y_scr[cb, pl.ds(r0, rows), :] = jnp.concatenate(accs, axis=0)
            return carry

        lax.fori_loop(0, ts // rows, row_block, 0)

    def plane(cb):
        return y_scr[cb, 0:ts, :]

    tot = plane(0)
    for cb in range(1, nblk):
        tot = tot + plane(cb)
    mu = jnp.sum(tot, axis=-1, keepdims=True) * (1.0 / CONV_WIDTH)
    sq = None
    for cb in range(nblk):
        d = plane(cb) - mu
        sq = d * d if sq is None else sq + d * d
    var = jnp.sum(sq, axis=-1, keepdims=True) * (1.0 / CONV_WIDTH)
    inv = lax.rsqrt(var + EPS)
    for cb in range(nblk):
        cs = slice(cb * lanes, (cb + 1) * lanes)
        z = (plane(cb) - mu) * inv * lg_ref[:, cs] + lb_ref[:, cs]
        o_ref[:, cs] = (z * _sigmoid(z)).astype(BF16)


def _conv_module(u, seq, dw_w, dw_b, ln_g, ln_b):
    n = u.shape[0]
    ts = TS_CONV
    tiles_per_seq = seq // ts
    hb = ts // HALO_ROWS
    last = n // HALO_ROWS - 1
    span = ts + 2 * HALO_ROWS - SUBLANES
    nblk = CONV_WIDTH // HEAD_DIM
    return pl.pallas_call(
        functools.partial(_conv_kernel, ts=ts, tiles_per_seq=tiles_per_seq),
        grid=(n // ts,),
        in_specs=[
            pl.BlockSpec((HALO_ROWS, CONV_WIDTH), lambda i: (jnp.maximum(i * hb - 1, 0), 0)),
            pl.BlockSpec((ts, CONV_WIDTH), lambda i: (i, 0)),
            pl.BlockSpec((HALO_ROWS, CONV_WIDTH), lambda i: (jnp.minimum((i + 1) * hb, last), 0)),
            _const_spec((CONV_TAPS, CONV_WIDTH)),
            _const_spec((1, CONV_WIDTH)),
            _const_spec((1, CONV_WIDTH)),
            _const_spec((1, CONV_WIDTH)),
        ],
        out_specs=pl.BlockSpec((ts, CONV_WIDTH), lambda i: (i, 0)),
        out_shape=jax.ShapeDtypeStruct((n, CONV_WIDTH), BF16),
        scratch_shapes=[
            pltpu.VMEM((nblk, ts + 2 * HALO_ROWS, HEAD_DIM), F32),
            pltpu.VMEM((nblk, SUBLANES - 1, span, HEAD_DIM), F32),
            pltpu.VMEM((nblk, ts + SUBLANES, HEAD_DIM), F32),
        ],
        compiler_params=pltpu.CompilerParams(
            dimension_semantics=("arbitrary",), vmem_limit_bytes=VMEM_LIMIT_BYTES),
        name="conv_module",
    )(u, u, u, dw_w, dw_b, ln_g, ln_b)


def _ffn_kernel(x_ref, a_ref, c_ref, wo_ref, g_ref, wg_ref, wu_ref, wd_ref, o_ref, h_scr):
    f = pl.program_id(1)

    @pl.when(f == 0)
    def _():
        mix = jnp.dot(a_ref[...], wo_ref[0:ATTN_WIDTH, :], preferred_element_type=F32)
        mix = mix + jnp.dot(c_ref[...], wo_ref[ATTN_WIDTH:, :], preferred_element_type=F32)
        x1 = x_ref[...] + mix
        o_ref[...] = x1
        ms = jnp.mean(x1 * x1, axis=-1, keepdims=True)
        h_scr[...] = (x1 * lax.rsqrt(ms + EPS) * g_ref[...]).astype(BF16)

    h = h_scr[...]
    g = jnp.dot(h, wg_ref[...], preferred_element_type=F32)
    u = jnp.dot(h, wu_ref[...], preferred_element_type=F32)
    act = (g * _sigmoid(g) * u).astype(BF16)
    o_ref[...] += jnp.dot(act, wd_ref[...], preferred_element_type=F32)


def _out_ffn(x2d, attn, conv, w_out, norm_g, w_gate, w_up, w_down):
    n = x2d.shape[0]
    tm, tf = TM_FFN, TF_FFN
    return pl.pallas_call(
        _ffn_kernel,
        grid=(n // tm, D_FF // tf),
        in_specs=[
            pl.BlockSpec((tm, D_MODEL), lambda i, f: (i, 0)),
            pl.BlockSpec((tm, ATTN_WIDTH), lambda i, f: (i, 0)),
            pl.BlockSpec((tm, CONV_WIDTH), lambda i, f: (i, 0)),
            _const_spec((D_MODEL, D_MODEL)),
            _const_spec((1, D_MODEL)),
            pl.BlockSpec((D_MODEL, tf), lambda i, f: (0, f)),
            pl.BlockSpec((D_MODEL, tf), lambda i, f: (0, f)),
            pl.BlockSpec((tf, D_MODEL), lambda i, f: (f, 0)),
        ],
        out_specs=pl.BlockSpec((tm, D_MODEL), lambda i, f: (i, 0)),
        out_shape=jax.ShapeDtypeStruct((n, D_MODEL), F32),
        scratch_shapes=[pltpu.VMEM((tm, D_MODEL), BF16)],
        compiler_params=pltpu.CompilerParams(
            dimension_semantics=("arbitrary", "arbitrary"), vmem_limit_bytes=VMEM_LIMIT_BYTES),
        name="out_ffn",
    )(x2d, attn, conv, w_out, norm_g, w_gate, w_up, w_down)


def _rope_tables(seq):
    rows = seq // GRID_W
    half = HEAD_DIM // 2
    inv_freq = ROPE_THETA ** (-jnp.arange(0, half, 2, dtype=F32) / half)
    ang_r = jnp.arange(rows).astype(F32)[:, None] * inv_freq[None, :]
    ang_c = jnp.arange(GRID_W).astype(F32)[:, None] * inv_freq[None, :]
    small = lax.optimization_barrier((jnp.cos(ang_r), jnp.sin(ang_r), jnp.cos(ang_c), jnp.sin(ang_c)))

    def expand(by_row, by_col):
        r = jnp.repeat(by_row, GRID_W, axis=0)
        c = jnp.tile(by_col, (rows, 1))
        return jnp.concatenate([r, r, c, c], axis=-1)

    cos, sin = expand(small[0], small[2]), expand(small[1], small[3])
    first = (jnp.arange(HEAD_DIM) // (HEAD_DIM // 4)) % 2 == 0
    sin_a = jnp.where(first[None, :], -sin, 0.0)
    sin_b = jnp.where(first[None, :], 0.0, sin)
    return (cos, sin_a, sin_b)


def _trunk(x, p):
    batch, seq, d_model = x.shape
    assert d_model == D_MODEL and x.dtype == F32
    assert seq % GRID_W == 0 and seq % TM_PROJ == 0 and seq % TS_CONV == 0
    assert seq % min(TK_ATTN, seq) == 0 and seq % min(QS_ATTN, seq) == 0 and min(QS_ATTN, seq) % TQ_ATTN == 0
    assert (batch * seq) % TM_FFN == 0 and TS_CONV % ROWS_CONV == 0 and D_FF % TF_FFN == 0
    x2d = x.reshape(batch * seq, D_MODEL)
    q, k, v, u = _in_projection(x2d, seq, p["norm_mix_g"], p["w_in"], p["b_glu"],
                                p["q_norm_g"], p["k_norm_g"], _rope_tables(seq))
    attn = _attention(q, k, v, batch, seq)
    conv = _conv_module(u, seq, p["dw_w"], p["dw_b"], p["conv_ln_g"], p["conv_ln_b"])
    y = _out_ffn(x2d, attn, conv, p["w_out"], p["norm_ffn_g"], p["w_gate"], p["w_up"], p["w_down"])
    return y.reshape(batch, seq, D_MODEL)


def _prepare(norm_mix_g, w_in, b_glu, q_norm_g, k_norm_g, dw_w, dw_b,
             conv_ln_g, conv_ln_b, w_out, norm_ffn_g, w_gate, w_up, w_down):
    assert norm_mix_g.shape[0] == 1, "single-layer trunk"
    return {
        "norm_mix_g": norm_mix_g[0][None, :],
        "w_in": w_in[0].astype(BF16),
        "b_glu": b_glu[0][None, :],
        "q_norm_g": q_norm_g[0][None, :],
        "k_norm_g": k_norm_g[0][None, :],
        "dw_w": dw_w[0],
        "dw_b": dw_b[0][None, :],
        "conv_ln_g": conv_ln_g[0][None, :],
        "conv_ln_b": conv_ln_b[0][None, :],
        "w_out": w_out[0].astype(BF16),
        "norm_ffn_g": norm_ffn_g[0][None, :],
        "w_gate": w_gate[0].astype(BF16),
        "w_up": w_up[0].astype(BF16),
        "w_down": w_down[0].astype(BF16),
    }


def kernel(x_prompt, x_sample, norm_mix_g, w_in, b_glu, q_norm_g, k_norm_g, dw_w, dw_b,
           conv_ln_g, conv_ln_b, w_out, norm_ffn_g, w_gate, w_up, w_down):
    p = _prepare(norm_mix_g, w_in, b_glu, q_norm_g, k_norm_g, dw_w, dw_b,
                 conv_ln_g, conv_ln_b, w_out, norm_ffn_g, w_gate, w_up, w_down)
    return (_trunk(x_prompt, p), _trunk(x_sample, p))
```

```python
import functools

import jax
import jax.numpy as jnp
from jax import lax
from jax.experimental import pallas as pl
from jax.experimental.pallas import tpu as pltpu

D_MODEL = 2048
GRID_W = 64
HEAD_DIM = 128
N_Q_HEADS = 8
N_KV_HEADS = 2
Q_PER_KV = N_Q_HEADS // N_KV_HEADS
ATTN_WIDTH = N_Q_HEADS * HEAD_DIM
KV_WIDTH = N_KV_HEADS * HEAD_DIM
CONV_WIDTH = D_MODEL - ATTN_WIDTH
CONV_TAPS = 31
CONV_PAD = CONV_TAPS // 2
IN_COLS = ATTN_WIDTH + 2 * KV_WIDTH + 2 * CONV_WIDTH
D_FF = 5632
ROPE_THETA = 10000.0
EPS = 1e-6
LOG2_E = 1.4426950408889634

F32 = jnp.float32
BF16 = jnp.bfloat16

V7X_VMEM_BYTES = 64 * 1024 * 1024
VMEM_LIMIT_BYTES = V7X_VMEM_BYTES - 8 * 1024 * 1024
SUBLANES = 8
BF16_ROWS = 16
HALO_ROWS = BF16_ROWS
assert HALO_ROWS >= CONV_PAD

TM_PROJ = 512
TQ_ATTN = 256
TK_ATTN = 2048
QS_ATTN = 4096
TS_CONV = 512
ROWS_CONV = 128
TM_FFN = 512
TF_FFN = 512


def _const_spec(shape):
    return pl.BlockSpec(shape, lambda *_: (0,) * len(shape), pipeline_mode=pl.Buffered(1))


def _sigmoid(x):
    return 1.0 / (1.0 + jnp.exp(-x))


def _inproj_kernel(x_ref, g_ref, w_ref, b_ref, qg_ref, kg_ref, cos_ref, sa_ref, sb_ref,
                   q_ref, k_ref, v_ref, u_ref, h_scr):
    x = x_ref[...]
    ms = jnp.mean(x * x, axis=-1, keepdims=True)
    h_scr[...] = (x * lax.rsqrt(ms + EPS) * g_ref[...]).astype(BF16)
    h = h_scr[...]

    def norm_rope(z, gain, c, sa, sb):
        z = z * lax.rsqrt(jnp.mean(z * z, axis=-1, keepdims=True) + EPS) * gain
        fwd = pltpu.roll(z, HEAD_DIM - HEAD_DIM // 4, 1)
        bwd = pltpu.roll(z, HEAD_DIM // 4, 1)
        return z * c + fwd * sa + bwd * sb

    cos, sa, sb = cos_ref[...], sa_ref[...], sb_ref[...]
    qg = qg_ref[...] * (HEAD_DIM ** -0.5 * LOG2_E)
    kg = kg_ref[...]

    half = ATTN_WIDTH // 2
    for c in range(2):
        z = jnp.dot(h, w_ref[:, c * half:(c + 1) * half], preferred_element_type=F32)
        for i in range(half // HEAD_DIM):
            zi = z[:, i * HEAD_DIM:(i + 1) * HEAD_DIM]
            q_ref[c * (half // HEAD_DIM) + i] = norm_rope(zi, qg, cos, sa, sb).astype(BF16)

    z = jnp.dot(h, w_ref[:, ATTN_WIDTH:ATTN_WIDTH + 2 * KV_WIDTH], preferred_element_type=F32)
    for i in range(N_KV_HEADS):
        zi = z[:, i * HEAD_DIM:(i + 1) * HEAD_DIM]
        k_ref[i] = norm_rope(zi, kg, cos, sa, sb).T.astype(BF16)
        v_ref[i, :, 0:HEAD_DIM] = z[:, KV_WIDTH + i * HEAD_DIM:KV_WIDTH + (i + 1) * HEAD_DIM].astype(BF16)
        v_ref[i, :, HEAD_DIM:] = jnp.ones((x.shape[0], HEAD_DIM), BF16)

    u0 = ATTN_WIDTH + 2 * KV_WIDTH
    cw = CONV_WIDTH // 2
    for c in range(2):
        a = jnp.dot(h, w_ref[:, u0 + c * cw:u0 + (c + 1) * cw], preferred_element_type=F32)
        a = a + b_ref[:, c * cw:(c + 1) * cw]
        gt = jnp.dot(h, w_ref[:, u0 + CONV_WIDTH + c * cw:u0 + CONV_WIDTH + (c + 1) * cw],
                     preferred_element_type=F32)
        gt = gt + b_ref[:, CONV_WIDTH + c * cw:CONV_WIDTH + (c + 1) * cw]
        u_ref[:, c * cw:(c + 1) * cw] = (a * _sigmoid(gt)).astype(BF16)


def _in_projection(x2d, seq, norm_g, w_in, b_glu, q_g, k_g, tables):
    n = x2d.shape[0]
    tm = TM_PROJ
    tiles_per_seq = seq // tm
    row = lambda i: (i, 0)
    tab = lambda i: (i % tiles_per_seq, 0)
    head_major = lambda i: (0, i, 0)
    tab_spec = pl.BlockSpec((tm, HEAD_DIM), tab)
    return pl.pallas_call(
        _inproj_kernel,
        grid=(n // tm,),
        in_specs=[
            pl.BlockSpec((tm, D_MODEL), row),
            _const_spec((1, D_MODEL)),
            _const_spec((D_MODEL, IN_COLS)),
            _const_spec((1, 2 * CONV_WIDTH)),
            _const_spec((1, HEAD_DIM)),
            _const_spec((1, HEAD_DIM)),
            tab_spec, tab_spec, tab_spec,
        ],
        out_specs=[
            pl.BlockSpec((N_Q_HEADS, tm, HEAD_DIM), head_major),
            pl.BlockSpec((N_KV_HEADS, HEAD_DIM, tm), lambda i: (0, 0, i)),
            pl.BlockSpec((N_KV_HEADS, tm, 2 * HEAD_DIM), head_major),
            pl.BlockSpec((tm, CONV_WIDTH), row),
        ],
        out_shape=[
            jax.ShapeDtypeStruct((N_Q_HEADS, n, HEAD_DIM), BF16),
            jax.ShapeDtypeStruct((N_KV_HEADS, HEAD_DIM, n), BF16),
            jax.ShapeDtypeStruct((N_KV_HEADS, n, 2 * HEAD_DIM), BF16),
            jax.ShapeDtypeStruct((n, CONV_WIDTH), BF16),
        ],
        scratch_shapes=[pltpu.VMEM((tm, D_MODEL), BF16)],
        compiler_params=pltpu.CompilerParams(
            dimension_semantics=("arbitrary",), vmem_limit_bytes=VMEM_LIMIT_BYTES),
        name="in_projection",
    )(x2d, norm_g, w_in, b_glu, q_g, k_g, *tables)


def _attn_kernel(q_ref, k_ref, v_ref, o_ref, s0, s1, m_scr, acc_scr, *, tq, tk, nq, nk):
    m_rows = Q_PER_KV * tq
    n_pairs = nq * nk
    s_buf = (s0, s1)

    def split(t):
        t = jnp.minimum(t, n_pairs - 1)
        return t // nk, t % nk

    def stage_a(t, par):
        qi, kj = split(t)
        q = q_ref[:, pl.ds(pl.multiple_of(qi * tq, tq), tq), :].reshape(m_rows, HEAD_DIM)
        k_t = k_ref[0, :, pl.ds(pl.multiple_of(kj * tk, tk), tk)]
        s_buf[par][...] = jnp.dot(q, k_t, preferred_element_type=F32)

    def stage_b(t, par):
        qi, kj = split(t)
        v = v_ref[0, pl.ds(pl.multiple_of(kj * tk, tk), tk), :]
        rows = pl.ds(pl.multiple_of(qi * tq, tq), tq)
        hp = 2 * tq
        for g2 in range(Q_PER_KV // 2):
            head = slice(g2 * hp, (g2 + 1) * hp)
            tile_max = jnp.max(s_buf[par][head, :], axis=1, keepdims=True)
            m_prev = jnp.where(kj == 0, -jnp.inf, m_scr[head, :])
            m_new = jnp.maximum(m_prev, tile_max)
            alpha = jnp.exp2(m_prev - m_new)
            m_scr[head, :] = m_new
            p = jnp.exp2(s_buf[par][head, :] - m_new).astype(BF16)
            acc = acc_scr[head, :] * alpha
            acc = acc + jnp.dot(p, v, preferred_element_type=F32)
            acc_scr[head, :] = acc
            o = acc[:, 0:HEAD_DIM] / acc[:, HEAD_DIM:HEAD_DIM + 1]
            for gg in range(2):
                g = 2 * g2 + gg
                o_ref[rows, g * HEAD_DIM:(g + 1) * HEAD_DIM] = o[gg * tq:(gg + 1) * tq].astype(BF16)

    m_scr[...] = jnp.zeros(m_scr.shape, F32)
    acc_scr[...] = jnp.zeros(acc_scr.shape, F32)
    stage_a(0, 0)

    def body(i, carry):
        for par in range(2):
            t = 2 * i + par
            stage_a(t + 1, 1 - par)
            stage_b(t, par)
        return carry

    lax.fori_loop(0, n_pairs // 2, body, 0)


def _attention(q, k, v, batch, seq):
    n = batch * seq
    tq = TQ_ATTN
    tk = min(TK_ATTN, seq)
    qs = min(QS_ATTN, seq)
    nq, nk, nsuper = qs // tq, seq // tk, seq // qs
    assert (nq * nk) % 2 == 0
    m = Q_PER_KV * tq
    kv_map = lambda b, h, si: (h, b, 0)
    return pl.pallas_call(
        functools.partial(_attn_kernel, tq=tq, tk=tk, nq=nq, nk=nk),
        grid=(batch, N_KV_HEADS, nsuper),
        in_specs=[
            pl.BlockSpec((Q_PER_KV, qs, HEAD_DIM), lambda b, h, si: (h, b * nsuper + si, 0)),
            pl.BlockSpec((1, HEAD_DIM, seq), lambda b, h, si: (h, 0, b)),
            pl.BlockSpec((1, seq, 2 * HEAD_DIM), kv_map),
        ],
        out_specs=pl.BlockSpec((qs, Q_PER_KV * HEAD_DIM), lambda b, h, si: (b * nsuper + si, h)),
        out_shape=jax.ShapeDtypeStruct((n, ATTN_WIDTH), BF16),
        scratch_shapes=[
            pltpu.VMEM((m, tk), F32),
            pltpu.VMEM((m, tk), F32),
            pltpu.VMEM((m, 1), F32),
            pltpu.VMEM((m, 2 * HEAD_DIM), F32),
        ],
        compiler_params=pltpu.CompilerParams(
            dimension_semantics=("arbitrary",) * 3, vmem_limit_bytes=VMEM_LIMIT_BYTES),
        name="attention",
    )(q, k, v)


def _conv_kernel(prev_ref, main_ref, next_ref, w_ref, b_ref, lg_ref, lb_ref, o_ref,
                 ext_scr, sh_scr, y_scr, *, ts, tiles_per_seq):
    si = pl.program_id(0) % tiles_per_seq
    halo = HALO_ROWS
    lanes = HEAD_DIM
    nblk = CONV_WIDTH // lanes
    span = ts + 2 * halo - SUBLANES
    base = halo - CONV_PAD
    rows = ROWS_CONV
    groups = rows // SUBLANES
    has_prev = si > 0
    has_next = si < tiles_per_seq - 1

    for cb in range(nblk):
        cs = slice(cb * lanes, (cb + 1) * lanes)
        ext_scr[cb, 0:halo, :] = jnp.where(has_prev, prev_ref[:, cs].astype(F32), 0.0)
        ext_scr[cb, halo:halo + ts, :] = main_ref[:, cs].astype(F32)
        ext_scr[cb, halo + ts:halo + ts + halo, :] = jnp.where(has_next, next_ref[:, cs].astype(F32), 0.0)
        for res in range(1, SUBLANES):
            sh_scr[cb, res - 1] = ext_scr[cb, res:res + span, :]
        taps = [jnp.broadcast_to(w_ref[j:j + 1, cs], (SUBLANES, lanes)) for j in range(CONV_TAPS)]
        bias = jnp.broadcast_to(b_ref[:, cs], (SUBLANES, lanes))

        def row_block(rb, carry, cb=cb, taps=taps, bias=bias):
            r0 = pl.multiple_of(rb * rows, rows)
            accs = [bias] * groups
            for j in range(CONV_TAPS):
                off = base + j
                res, start = off % SUBLANES, r0 + (off // SUBLANES) * SUBLANES
                if res == 0:
                    win = ext_scr[cb, pl.ds(start, rows), :]
                else:
                    win = sh_scr[cb, res - 1, pl.ds(start, rows), :]
                for r in range(groups):
                    accs[r] = accs[r] + win[r * SUBLANES:(r + 1) * SUBLANES] * taps[j]
            y_scr[cb, pl.ds(r0, rows), :] = jnp.concatenate(accs, axis=0)
            return carry

        lax.fori_loop(0, ts // rows, row_block, 0)

    def plane(cb):
        return y_scr[cb, 0:ts, :]

    tot = plane(0)
    for cb in range(1, nblk):
        tot = tot + plane(cb)
    mu = jnp.sum(tot, axis=-1, keepdims=True) * (1.0 / CONV_WIDTH)
    sq = None
    for cb in range(nblk):
        d = plane(cb) - mu
        sq = d * d if sq is None else sq + d * d
    var = jnp.sum(sq, axis=-1, keepdims=True) * (1.0 / CONV_WIDTH)
    inv = lax.rsqrt(var + EPS)
    for cb in range(nblk):
        cs = slice(cb * lanes, (cb + 1) * lanes)
        z = (plane(cb) - mu) * inv * lg_ref[:, cs] + lb_ref[:, cs]
        o_ref[:, cs] = (z * _sigmoid(z)).astype(BF16)


def _conv_module(u, seq, dw_w, dw_b, ln_g, ln_b):
    n = u.shape[0]
    ts = TS_CONV
    tiles_per_seq = seq // ts
    hb = ts // HALO_ROWS
    last = n // HALO_ROWS - 1
    span = ts + 2 * HALO_ROWS - SUBLANES
    nblk = CONV_WIDTH // HEAD_DIM
    return pl.pallas_call(
        functools.partial(_conv_kernel, ts=ts, tiles_per_seq=tiles_per_seq),
        grid=(n // ts,),
        in_specs=[
            pl.BlockSpec((HALO_ROWS, CONV_WIDTH), lambda i: (jnp.maximum(i * hb - 1, 0), 0)),
            pl.BlockSpec((ts, CONV_WIDTH), lambda i: (i, 0)),
            pl.BlockSpec((HALO_ROWS, CONV_WIDTH), lambda i: (jnp.minimum((i + 1) * hb, last), 0)),
            _const_spec((CONV_TAPS, CONV_WIDTH)),
            _const_spec((1, CONV_WIDTH)),
            _const_spec((1, CONV_WIDTH)),
            _const_spec((1, CONV_WIDTH)),
        ],
        out_specs=pl.BlockSpec((ts, CONV_WIDTH), lambda i: (i, 0)),
        out_shape=jax.ShapeDtypeStruct((n, CONV_WIDTH), BF16),
        scratch_shapes=[
            pltpu.VMEM((nblk, ts + 2 * HALO_ROWS, HEAD_DIM), F32),
            pltpu.VMEM((nblk, SUBLANES - 1, span, HEAD_DIM), F32),
            pltpu.VMEM((nblk, ts + SUBLANES, HEAD_DIM), F32),
        ],
        compiler_params=pltpu.CompilerParams(
            dimension_semantics=("arbitrary",), vmem_limit_bytes=VMEM_LIMIT_BYTES),
        name="conv_module",
    )(u, u, u, dw_w, dw_b, ln_g, ln_b)


def _ffn_kernel(x_ref, a_ref, c_ref, wo_ref, g_ref, wg_ref, wu_ref, wd_ref, o_ref, h_scr):
    f = pl.program_id(1)

    @pl.when(f == 0)
    def _():
        mix = jnp.dot(a_ref[...], wo_ref[0:ATTN_WIDTH, :], preferred_element_type=F32)
        mix = mix + jnp.dot(c_ref[...], wo_ref[ATTN_WIDTH:, :], preferred_element_type=F32)
        x1 = x_ref[...] + mix
        o_ref[...] = x1
        ms = jnp.mean(x1 * x1, axis=-1, keepdims=True)
        h_scr[...] = (x1 * lax.rsqrt(ms + EPS) * g_ref[...]).astype(BF16)

    h = h_scr[...]
    g = jnp.dot(h, wg_ref[...], preferred_element_type=F32)
    u = jnp.dot(h, wu_ref[...], preferred_element_type=F32)
    act = (g * _sigmoid(g) * u).astype(BF16)
    o_ref[...] += jnp.dot(act, wd_ref[...], preferred_element_type=F32)


def _out_ffn(x2d, attn, conv, w_out, norm_g, w_gate, w_up, w_down):
    n = x2d.shape[0]
    tm, tf = TM_FFN, TF_FFN
    return pl.pallas_call(
        _ffn_kernel,
        grid=(n // tm, D_FF // tf),
        in_specs=[
            pl.BlockSpec((tm, D_MODEL), lambda i, f: (i, 0)),
            pl.BlockSpec((tm, ATTN_WIDTH), lambda i, f: (i, 0)),
            pl.BlockSpec((tm, CONV_WIDTH), lambda i, f: (i, 0)),
            _const_spec((D_MODEL, D_MODEL)),
            _const_spec((1, D_MODEL)),
            pl.BlockSpec((D_MODEL, tf), lambda i, f: (0, f)),
            pl.BlockSpec((D_MODEL, tf), lambda i, f: (0, f)),
            pl.BlockSpec((tf, D_MODEL), lambda i, f: (f, 0)),
        ],
        out_specs=pl.BlockSpec((tm, D_MODEL), lambda i, f: (i, 0)),
        out_shape=jax.ShapeDtypeStruct((n, D_MODEL), F32),
        scratch_shapes=[pltpu.VMEM((tm, D_MODEL), BF16)],
        compiler_params=pltpu.CompilerParams(
            dimension_semantics=("arbitrary", "arbitrary"), vmem_limit_bytes=VMEM_LIMIT_BYTES),
        name="out_ffn",
    )(x2d, attn, conv, w_out, norm_g, w_gate, w_up, w_down)


def _rope_tables(seq):
    rows = seq // GRID_W
    half = HEAD_DIM // 2
    inv_freq = ROPE_THETA ** (-jnp.arange(0, half, 2, dtype=F32) / half)
    ang_r = jnp.arange(rows).astype(F32)[:, None] * inv_freq[None, :]
    ang_c = jnp.arange(GRID_W).astype(F32)[:, None] * inv_freq[None, :]
    small = lax.optimization_barrier((jnp.cos(ang_r), jnp.sin(ang_r), jnp.cos(ang_c), jnp.sin(ang_c)))

    def expand(by_row, by_col):
        r = jnp.repeat(by_row, GRID_W, axis=0)
        c = jnp.tile(by_col, (rows, 1))
        return jnp.concatenate([r, r, c, c], axis=-1)

    cos, sin = expand(small[0], small[2]), expand(small[1], small[3])
    first = (jnp.arange(HEAD_DIM) // (HEAD_DIM // 4)) % 2 == 0
    sin_a = jnp.where(first[None, :], -sin, 0.0)
    sin_b = jnp.where(first[None, :], 0.0, sin)
    return (cos, sin_a, sin_b)


def _trunk(x, p):
    batch, seq, d_model = x.shape
    assert d_model == D_MODEL and x.dtype == F32
    assert seq % GRID_W == 0 and seq % TM_PROJ == 0 and seq % TS_CONV == 0
    assert seq % min(TK_ATTN, seq) == 0 and seq % min(QS_ATTN, seq) == 0 and min(QS_ATTN, seq) % TQ_ATTN == 0
    assert (batch * seq) % TM_FFN == 0 and TS_CONV % ROWS_CONV == 0 and D_FF % TF_FFN == 0
    x2d = x.reshape(batch * seq, D_MODEL)
    q, k, v, u = _in_projection(x2d, seq, p["norm_mix_g"], p["w_in"], p["b_glu"],
                                p["q_norm_g"], p["k_norm_g"], _rope_tables(seq))
    attn = _attention(q, k, v, batch, seq)
    conv = _conv_module(u, seq, p["dw_w"], p["dw_b"], p["conv_ln_g"], p["conv_ln_b"])
    y = _out_ffn(x2d, attn, conv, p["w_out"], p["norm_ffn_g"], p["w_gate"], p["w_up"], p["w_down"])
    return y.reshape(batch, seq, D_MODEL)


def _prepare(norm_mix_g, w_in, b_glu, q_norm_g, k_norm_g, dw_w, dw_b,
             conv_ln_g, conv_ln_b, w_out, norm_ffn_g, w_gate, w_up, w_down):
    assert norm_mix_g.shape[0] == 1, "single-layer trunk"
    return {
        "norm_mix_g": norm_mix_g[0][None, :],
        "w_in": w_in[0].astype(BF16),
        "b_glu": b_glu[0][None, :],
        "q_norm_g": q_norm_g[0][None, :],
        "k_norm_g": k_norm_g[0][None, :],
        "dw_w": dw_w[0],
        "dw_b": dw_b[0][None, :],
        "conv_ln_g": conv_ln_g[0][None, :],
        "conv_ln_b": conv_ln_b[0][None, :],
        "w_out": w_out[0].astype(BF16),
        "norm_ffn_g": norm_ffn_g[0][None, :],
        "w_gate": w_gate[0].astype(BF16),
        "w_up": w_up[0].astype(BF16),
        "w_down": w_down[0].astype(BF16),
    }


def kernel(x_prompt, x_sample, norm_mix_g, w_in, b_glu, q_norm_g, k_norm_g, dw_w, dw_b,
           conv_ln_g, conv_ln_b, w_out, norm_ffn_g, w_gate, w_up, w_down):
    p = _prepare(norm_mix_g, w_in, b_glu, q_norm_g, k_norm_g, dw_w, dw_b,
                 conv_ln_g, conv_ln_b, w_out, norm_ffn_g, w_gate, w_up, w_down)
    return (_trunk(x_prompt, p), _trunk(x_sample, p))
```

```python
import functools

import jax
import jax.numpy as jnp
from jax import lax
from jax.experimental import pallas as pl
from jax.experimental.pallas import tpu as pltpu

D_MODEL = 2048
GRID_W = 64
HEAD_DIM = 128
N_Q_HEADS = 8
N_KV_HEADS = 2
Q_PER_KV = N_Q_HEADS // N_KV_HEADS
ATTN_WIDTH = N_Q_HEADS * HEAD_DIM
KV_WIDTH = N_KV_HEADS * HEAD_DIM
CONV_WIDTH = D_MODEL - ATTN_WIDTH
CONV_TAPS = 31
CONV_PAD = CONV_TAPS // 2
IN_COLS = ATTN_WIDTH + 2 * KV_WIDTH + 2 * CONV_WIDTH
D_FF = 5632
ROPE_THETA = 10000.0
EPS = 1e-6
LOG2_E = 1.4426950408889634

F32 = jnp.float32
BF16 = jnp.bfloat16

V7X_VMEM_BYTES = 64 * 1024 * 1024
VMEM_LIMIT_BYTES = V7X_VMEM_BYTES - 8 * 1024 * 1024
SUBLANES = 8
BF16_ROWS = 16
HALO_ROWS = BF16_ROWS
assert HALO_ROWS >= CONV_PAD

TM_PROJ = 512
TQ_ATTN = 512
TK_ATTN = 1024
QS_ATTN = 4096
TS_CONV = 512
ROWS_CONV = 128
TM_FFN = 512
TF_FFN = 512


def _const_spec(shape):
    return pl.BlockSpec(shape, lambda *_: (0,) * len(shape), pipeline_mode=pl.Buffered(1))


def _sigmoid(x):
    return 1.0 / (1.0 + jnp.exp(-x))


def _inproj_kernel(x_ref, g_ref, w_ref, b_ref, qg_ref, kg_ref, cos_ref, sa_ref, sb_ref,
                   q_ref, k_ref, v_ref, u_ref, h_scr):
    x = x_ref[...]
    ms = jnp.mean(x * x, axis=-1, keepdims=True)
    h_scr[...] = (x * lax.rsqrt(ms + EPS) * g_ref[...]).astype(BF16)
    h = h_scr[...]

    def norm_rope(z, gain, c, sa, sb):
        z = z * lax.rsqrt(jnp.mean(z * z, axis=-1, keepdims=True) + EPS) * gain
        fwd = pltpu.roll(z, HEAD_DIM - HEAD_DIM // 4, 1)
        bwd = pltpu.roll(z, HEAD_DIM // 4, 1)
        return z * c + fwd * sa + bwd * sb

    cos, sa, sb = cos_ref[...], sa_ref[...], sb_ref[...]
    qg = qg_ref[...] * (HEAD_DIM ** -0.5 * LOG2_E)
    kg = kg_ref[...]

    half = ATTN_WIDTH // 2
    for c in range(2):
        z = jnp.dot(h, w_ref[:, c * half:(c + 1) * half], preferred_element_type=F32)
        for i in range(half // HEAD_DIM):
            zi = z[:, i * HEAD_DIM:(i + 1) * HEAD_DIM]
            q_ref[c * (half // HEAD_DIM) + i] = norm_rope(zi, qg, cos, sa, sb).astype(BF16)

    z = jnp.dot(h, w_ref[:, ATTN_WIDTH:ATTN_WIDTH + 2 * KV_WIDTH], preferred_element_type=F32)
    for i in range(N_KV_HEADS):
        zi = z[:, i * HEAD_DIM:(i + 1) * HEAD_DIM]
        k_ref[i] = norm_rope(zi, kg, cos, sa, sb).T.astype(BF16)
        v_ref[i, :, 0:HEAD_DIM] = z[:, KV_WIDTH + i * HEAD_DIM:KV_WIDTH + (i + 1) * HEAD_DIM].astype(BF16)
        v_ref[i, :, HEAD_DIM:] = jnp.ones((x.shape[0], HEAD_DIM), BF16)

    u0 = ATTN_WIDTH + 2 * KV_WIDTH
    cw = CONV_WIDTH // 2
    for c in range(2):
        a = jnp.dot(h, w_ref[:, u0 + c * cw:u0 + (c + 1) * cw], preferred_element_type=F32)
        a = a + b_ref[:, c * cw:(c + 1) * cw]
        gt = jnp.dot(h, w_ref[:, u0 + CONV_WIDTH + c * cw:u0 + CONV_WIDTH + (c + 1) * cw],
                     preferred_element_type=F32)
        gt = gt + b_ref[:, CONV_WIDTH + c * cw:CONV_WIDTH + (c + 1) * cw]
        u_ref[:, c * cw:(c + 1) * cw] = (a * _sigmoid(gt)).astype(BF16)


def _in_projection(x2d, seq, norm_g, w_in, b_glu, q_g, k_g, tables):
    n = x2d.shape[0]
    tm = TM_PROJ
    tiles_per_seq = seq // tm
    row = lambda i: (i, 0)
    tab = lambda i: (i % tiles_per_seq, 0)
    head_major = lambda i: (0, i, 0)
    tab_spec = pl.BlockSpec((tm, HEAD_DIM), tab)
    return pl.pallas_call(
        _inproj_kernel,
        grid=(n // tm,),
        in_specs=[
            pl.BlockSpec((tm, D_MODEL), row),
            _const_spec((1, D_MODEL)),
            _const_spec((D_MODEL, IN_COLS)),
            _const_spec((1, 2 * CONV_WIDTH)),
            _const_spec((1, HEAD_DIM)),
            _const_spec((1, HEAD_DIM)),
            tab_spec, tab_spec, tab_spec,
        ],
        out_specs=[
            pl.BlockSpec((N_Q_HEADS, tm, HEAD_DIM), head_major),
            pl.BlockSpec((N_KV_HEADS, HEAD_DIM, tm), lambda i: (0, 0, i)),
            pl.BlockSpec((N_KV_HEADS, tm, 2 * HEAD_DIM), head_major),
            pl.BlockSpec((tm, CONV_WIDTH), row),
        ],
        out_shape=[
            jax.ShapeDtypeStruct((N_Q_HEADS, n, HEAD_DIM), BF16),
            jax.ShapeDtypeStruct((N_KV_HEADS, HEAD_DIM, n), BF16),
            jax.ShapeDtypeStruct((N_KV_HEADS, n, 2 * HEAD_DIM), BF16),
            jax.ShapeDtypeStruct((n, CONV_WIDTH), BF16),
        ],
        scratch_shapes=[pltpu.VMEM((tm, D_MODEL), BF16)],
        compiler_params=pltpu.CompilerParams(
            dimension_semantics=("arbitrary",), vmem_limit_bytes=VMEM_LIMIT_BYTES),
        name="in_projection",
    )(x2d, norm_g, w_in, b_glu, q_g, k_g, *tables)


def _attn_kernel(q_ref, k_ref, v_ref, o_ref, s0, s1, m_scr, acc_scr, *, tq, tk, nq, nk):
    m_rows = Q_PER_KV * tq
    n_pairs = nq * nk
    s_buf = (s0, s1)

    def split(t):
        t = jnp.minimum(t, n_pairs - 1)
        return t // nk, t % nk

    def stage_a(t, par):
        qi, kj = split(t)
        q = q_ref[:, pl.ds(pl.multiple_of(qi * tq, tq), tq), :].reshape(m_rows, HEAD_DIM)
        k_t = k_ref[0, :, pl.ds(pl.multiple_of(kj * tk, tk), tk)]
        s_buf[par][...] = jnp.dot(q, k_t, preferred_element_type=F32)

    def stage_b(t, par):
        qi, kj = split(t)
        v = v_ref[0, pl.ds(pl.multiple_of(kj * tk, tk), tk), :]
        rows = pl.ds(pl.multiple_of(qi * tq, tq), tq)
        for g in range(Q_PER_KV):
            head = slice(g * tq, (g + 1) * tq)
            tile_max = jnp.max(s_buf[par][head, :], axis=1, keepdims=True)
            m_prev = jnp.where(kj == 0, -jnp.inf, m_scr[head, :])
            m_new = jnp.maximum(m_prev, tile_max)
            alpha = jnp.exp2(m_prev - m_new)
            m_scr[head, :] = m_new
            p = jnp.exp2(s_buf[par][head, :] - m_new).astype(BF16)
            acc = acc_scr[head, :] * alpha
            acc = acc + jnp.dot(p, v, preferred_element_type=F32)
            acc_scr[head, :] = acc
            o = acc[:, 0:HEAD_DIM] / acc[:, HEAD_DIM:HEAD_DIM + 1]
            o_ref[rows, g * HEAD_DIM:(g + 1) * HEAD_DIM] = o.astype(BF16)

    m_scr[...] = jnp.zeros(m_scr.shape, F32)
    acc_scr[...] = jnp.zeros(acc_scr.shape, F32)
    stage_a(0, 0)

    def body(i, carry):
        for par in range(2):
            t = 2 * i + par
            stage_a(t + 1, 1 - par)
            stage_b(t, par)
        return carry

    lax.fori_loop(0, n_pairs // 2, body, 0)


def _attention(q, k, v, batch, seq):
    n = batch * seq
    tq = TQ_ATTN
    tk = min(TK_ATTN, seq)
    qs = min(QS_ATTN, seq)
    nq, nk, nsuper = qs // tq, seq // tk, seq // qs
    assert (nq * nk) % 2 == 0
    m = Q_PER_KV * tq
    kv_map = lambda b, h, si: (h, b, 0)
    return pl.pallas_call(
        functools.partial(_attn_kernel, tq=tq, tk=tk, nq=nq, nk=nk),
        grid=(batch, N_KV_HEADS, nsuper),
        in_specs=[
            pl.BlockSpec((Q_PER_KV, qs, HEAD_DIM), lambda b, h, si: (h, b * nsuper + si, 0)),
            pl.BlockSpec((1, HEAD_DIM, seq), lambda b, h, si: (h, 0, b)),
            pl.BlockSpec((1, seq, 2 * HEAD_DIM), kv_map),
        ],
        out_specs=pl.BlockSpec((qs, Q_PER_KV * HEAD_DIM), lambda b, h, si: (b * nsuper + si, h)),
        out_shape=jax.ShapeDtypeStruct((n, ATTN_WIDTH), BF16),
        scratch_shapes=[
            pltpu.VMEM((m, tk), F32),
            pltpu.VMEM((m, tk), F32),
            pltpu.VMEM((m, 1), F32),
            pltpu.VMEM((m, 2 * HEAD_DIM), F32),
        ],
        compiler_params=pltpu.CompilerParams(
            dimension_semantics=("arbitrary",) * 3, vmem_limit_bytes=VMEM_LIMIT_BYTES),
        name="attention",
    )(q, k, v)


def _conv_kernel(prev_ref, main_ref, next_ref, w_ref, b_ref, lg_ref, lb_ref, o_ref,
                 ext_scr, sh_scr, y_scr, *, ts, tiles_per_seq):
    si = pl.program_id(0) % tiles_per_seq
    halo = HALO_ROWS
    lanes = HEAD_DIM
    nblk = CONV_WIDTH // lanes
    span = ts + 2 * halo - SUBLANES
    base = halo - CONV_PAD
    rows = ROWS_CONV
    groups = rows // SUBLANES
    has_prev = si > 0
    has_next = si < tiles_per_seq - 1

    for cb in range(nblk):
        cs = slice(cb * lanes, (cb + 1) * lanes)
        ext_scr[cb, 0:halo, :] = jnp.where(has_prev, prev_ref[:, cs].astype(F32), 0.0)
        ext_scr[cb, halo:halo + ts, :] = main_ref[:, cs].astype(F32)
        ext_scr[cb, halo + ts:halo + ts + halo, :] = jnp.where(has_next, next_ref[:, cs].astype(F32), 0.0)
        for res in range(1, SUBLANES):
            sh_scr[cb, res - 1] = ext_scr[cb, res:res + span, :]
        taps = [jnp.broadcast_to(w_ref[j:j + 1, cs], (SUBLANES, lanes)) for j in range(CONV_TAPS)]
        bias = jnp.broadcast_to(b_ref[:, cs], (SUBLANES, lanes))

        def row_block(rb, carry, cb=cb, taps=taps, bias=bias):
            r0 = pl.multiple_of(rb * rows, rows)
            accs = [bias] * groups
            for j in range(CONV_TAPS):
                off = base + j
                res, start = off % SUBLANES, r0 + (off // SUBLANES) * SUBLANES
                if res == 0:
                    win = ext_scr[cb, pl.ds(start, rows), :]
                else:
                    win = sh_scr[cb, res - 1, pl.ds(start, rows), :]
                for r in range(groups):
                    accs[r] = accs[r] + win[r * SUBLANES:(r + 1) * SUBLANES] * taps[j]
            y_scr[cb, pl.ds(r0, rows), :] = jnp.concatenate(accs, axis=0)
            return carry

        lax.fori_loop(0, ts // rows, row_block, 0)

    def plane(cb):
        return y_scr[cb, 0:ts, :]

    tot = plane(0)
    for cb in range(1, nblk):
        tot = tot + plane(cb)
    mu = jnp.sum(tot, axis=-1, keepdims=True) * (1.0 / CONV_WIDTH)
    sq = None
    for cb in range(nblk):
        d = plane(cb) - mu
        sq = d * d if sq is None else sq + d * d
    var = jnp.sum(sq, axis=-1, keepdims=True) * (1.0 / CONV_WIDTH)
    inv = lax.rsqrt(var + EPS)
    for cb in range(nblk):
        cs = slice(cb * lanes, (cb + 1) * lanes)
        z = (plane(cb) - mu) * inv * lg_ref[:, cs] + lb_ref[:, cs]
        o_ref[:, cs] = (z * _sigmoid(z)).astype(BF16)


def _conv_module(u, seq, dw_w, dw_b, ln_g, ln_b):
    n = u.shape[0]
    ts = TS_CONV
    tiles_per_seq = seq // ts
    hb = ts // HALO_ROWS
    last = n // HALO_ROWS - 1
    span = ts + 2 * HALO_ROWS - SUBLANES
    nblk = CONV_WIDTH // HEAD_DIM
    return pl.pallas_call(
        functools.partial(_conv_kernel, ts=ts, tiles_per_seq=tiles_per_seq),
        grid=(n // ts,),
        in_specs=[
            pl.BlockSpec((HALO_ROWS, CONV_WIDTH), lambda i: (jnp.maximum(i * hb - 1, 0), 0)),
            pl.BlockSpec((ts, CONV_WIDTH), lambda i: (i, 0)),
            pl.BlockSpec((HALO_ROWS, CONV_WIDTH), lambda i: (jnp.minimum((i + 1) * hb, last), 0)),
            _const_spec((CONV_TAPS, CONV_WIDTH)),
            _const_spec((1, CONV_WIDTH)),
            _const_spec((1, CONV_WIDTH)),
            _const_spec((1, CONV_WIDTH)),
        ],
        out_specs=pl.BlockSpec((ts, CONV_WIDTH), lambda i: (i, 0)),
        out_shape=jax.ShapeDtypeStruct((n, CONV_WIDTH), BF16),
        scratch_shapes=[
            pltpu.VMEM((nblk, ts + 2 * HALO_ROWS, HEAD_DIM), F32),
            pltpu.VMEM((nblk, SUBLANES - 1, span, HEAD_DIM), F32),
            pltpu.VMEM((nblk, ts + SUBLANES, HEAD_DIM), F32),
        ],
        compiler_params=pltpu.CompilerParams(
            dimension_semantics=("arbitrary",), vmem_limit_bytes=VMEM_LIMIT_BYTES),
        name="conv_module",
    )(u, u, u, dw_w, dw_b, ln_g, ln_b)


def _ffn_kernel(x_ref, a_ref, c_ref, wo_ref, g_ref, wg_ref, wu_ref, wd_ref, o_ref, h_scr):
    f = pl.program_id(1)

    @pl.when(f == 0)
    def _():
        mix = jnp.dot(a_ref[...], wo_ref[0:ATTN_WIDTH, :], preferred_element_type=F32)
        mix = mix + jnp.dot(c_ref[...], wo_ref[ATTN_WIDTH:, :], preferred_element_type=F32)
        x1 = x_ref[...] + mix
        o_ref[...] = x1
        ms = jnp.mean(x1 * x1, axis=-1, keepdims=True)
        h_scr[...] = (x1 * lax.rsqrt(ms + EPS) * g_ref[...]).astype(BF16)

    h = h_scr[...]
    g = jnp.dot(h, wg_ref[...], preferred_element_type=F32)
    u = jnp.dot(h, wu_ref[...], preferred_element_type=F32)
    act = (g * _sigmoid(g) * u).astype(BF16)
    o_ref[...] += jnp.dot(act, wd_ref[...], preferred_element_type=F32)


def _out_ffn(x2d, attn, conv, w_out, norm_g, w_gate, w_up, w_down):
    n = x2d.shape[0]
    tm, tf = TM_FFN, TF_FFN
    return pl.pallas_call(
        _ffn_kernel,
        grid=(n // tm, D_FF // tf),
        in_specs=[
            pl.BlockSpec((tm, D_MODEL), lambda i, f: (i, 0)),
            pl.BlockSpec((tm, ATTN_WIDTH), lambda i, f: (i, 0)),
            pl.BlockSpec((tm, CONV_WIDTH), lambda i, f: (i, 0)),
            _const_spec((D_MODEL, D_MODEL)),
            _const_spec((1, D_MODEL)),
            pl.BlockSpec((D_MODEL, tf), lambda i, f: (0, f)),
            pl.BlockSpec((D_MODEL, tf), lambda i, f: (0, f)),
            pl.BlockSpec((tf, D_MODEL), lambda i, f: (f, 0)),
        ],
        out_specs=pl.BlockSpec((tm, D_MODEL), lambda i, f: (i, 0)),
        out_shape=jax.ShapeDtypeStruct((n, D_MODEL), F32),
        scratch_shapes=[pltpu.VMEM((tm, D_MODEL), BF16)],
        compiler_params=pltpu.CompilerParams(
            dimension_semantics=("arbitrary", "arbitrary"), vmem_limit_bytes=VMEM_LIMIT_BYTES),
        name="out_ffn",
    )(x2d, attn, conv, w_out, norm_g, w_gate, w_up, w_down)


def _rope_tables(seq):
    rows = seq // GRID_W
    half = HEAD_DIM // 2
    inv_freq = ROPE_THETA ** (-jnp.arange(0, half, 2, dtype=F32) / half)
    ang_r = jnp.arange(rows).astype(F32)[:, None] * inv_freq[None, :]
    ang_c = jnp.arange(GRID_W).astype(F32)[:, None] * inv_freq[None, :]
    small = lax.optimization_barrier((jnp.cos(ang_r), jnp.sin(ang_r), jnp.cos(ang_c), jnp.sin(ang_c)))

    def expand(by_row, by_col):
        r = jnp.repeat(by_row, GRID_W, axis=0)
        c = jnp.tile(by_col, (rows, 1))
        return jnp.concatenate([r, r, c, c], axis=-1)

    cos, sin = expand(small[0], small[2]), expand(small[1], small[3])
    first = (jnp.arange(HEAD_DIM) // (HEAD_DIM // 4)) % 2 == 0
    sin_a = jnp.where(first[None, :], -sin, 0.0)
    sin_b = jnp.where(first[None, :], 0.0, sin)
    return (cos, sin_a, sin_b)


def _trunk(x, p):
    batch, seq, d_model = x.shape
    assert d_model == D_MODEL and x.dtype == F32
    assert seq % GRID_W == 0 and seq % TM_PROJ == 0 and seq % TS_CONV == 0
    assert seq % min(TK_ATTN, seq) == 0 and seq % min(QS_ATTN, seq) == 0 and min(QS_ATTN, seq) % TQ_ATTN == 0
    assert (batch * seq) % TM_FFN == 0 and TS_CONV % ROWS_CONV == 0 and D_FF % TF_FFN == 0
    x2d = x.reshape(batch * seq, D_MODEL)
    q, k, v, u = _in_projection(x2d, seq, p["norm_mix_g"], p["w_in"], p["b_glu"],
                                p["q_norm_g"], p["k_norm_g"], _rope_tables(seq))
    attn = _attention(q, k, v, batch, seq)
    conv = _conv_module(u, seq, p["dw_w"], p["dw_b"], p["conv_ln_g"], p["conv_ln_b"])
    y = _out_ffn(x2d, attn, conv, p["w_out"], p["norm_ffn_g"], p["w_gate"], p["w_up"], p["w_down"])
    return y.reshape(batch, seq, D_MODEL)


def _prepare(norm_mix_g, w_in, b_glu, q_norm_g, k_norm_g, dw_w, dw_b,
             conv_ln_g, conv_ln_b, w_out, norm_ffn_g, w_gate, w_up, w_down):
    assert norm_mix_g.shape[0] == 1, "single-layer trunk"
    return {
        "norm_mix_g": norm_mix_g[0][None, :],
        "w_in": w_in[0].astype(BF16),
        "b_glu": b_glu[0][None, :],
        "q_norm_g": q_norm_g[0][None, :],
        "k_norm_g": k_norm_g[0][None, :],
        "dw_w": dw_w[0],
        "dw_b": dw_b[0][None, :],
        "conv_ln_g": conv_ln_g[0][None, :],
        "conv_ln_b": conv_ln_b[0][None, :],
        "w_out": w_out[0].astype(BF16),
        "norm_ffn_g": norm_ffn_g[0][None, :],
        "w_gate": w_gate[0].astype(BF16),
        "w_up": w_up[0].astype(BF16),
        "w_down": w_down[0].astype(BF16),
    }


def kernel(x_prompt, x_sample, norm_mix_g, w_in, b_glu, q_norm_g, k_norm_g, dw_w, dw_b,
           conv_ln_g, conv_ln_b, w_out, norm_ffn_g, w_gate, w_up, w_down):
    p = _prepare(norm_mix_g, w_in, b_glu, q_norm_g, k_norm_g, dw_w, dw_b,
                 conv_ln_g, conv_ln_b, w_out, norm_ffn_g, w_gate, w_up, w_down)
    return (_trunk(x_prompt, p), _trunk(x_sample, p))
```

```python
import functools

import jax
import jax.numpy as jnp
from jax import lax
from jax.experimental import pallas as pl
from jax.experimental.pallas import tpu as pltpu

D_MODEL = 2048
GRID_W = 64
HEAD_DIM = 128
N_Q_HEADS = 8
N_KV_HEADS = 2
Q_PER_KV = N_Q_HEADS // N_KV_HEADS
ATTN_WIDTH = N_Q_HEADS * HEAD_DIM
KV_WIDTH = N_KV_HEADS * HEAD_DIM
CONV_WIDTH = D_MODEL - ATTN_WIDTH
CONV_TAPS = 31
CONV_PAD = CONV_TAPS // 2
IN_COLS = ATTN_WIDTH + 2 * KV_WIDTH + 2 * CONV_WIDTH
D_FF = 5632
ROPE_THETA = 10000.0
EPS = 1e-6
LOG2_E = 1.4426950408889634

F32 = jnp.float32
BF16 = jnp.bfloat16

V7X_VMEM_BYTES = 64 * 1024 * 1024
VMEM_LIMIT_BYTES = V7X_VMEM_BYTES - 8 * 1024 * 1024
SUBLANES = 8
BF16_ROWS = 16
HALO_ROWS = BF16_ROWS
assert HALO_ROWS >= CONV_PAD

TM_PROJ = 512
TQ_ATTN = 256
TK_ATTN = 2048
QS_ATTN = 4096
TS_CONV = 512
ROWS_CONV = 128
TM_FFN = 512
TF_FFN = 512


def _const_spec(shape):
    return pl.BlockSpec(shape, lambda *_: (0,) * len(shape), pipeline_mode=pl.Buffered(1))


def _sigmoid(x):
    return 1.0 / (1.0 + jnp.exp(-x))


def _inproj_kernel(x_ref, g_ref, w_ref, b_ref, qg_ref, kg_ref, cos_ref, sa_ref, sb_ref, *rest, n_cast):
    cast_in, (q_ref, k_ref, v_ref, u_ref), cast_out, h_scr = (
        rest[:n_cast], rest[n_cast:n_cast + 4], rest[n_cast + 4:2 * n_cast + 4], rest[2 * n_cast + 4])
    for src, dst in zip(cast_in, cast_out):
        dst[...] = src[...].astype(BF16)
    x = x_ref[...]
    ms = jnp.mean(x * x, axis=-1, keepdims=True)
    h_scr[...] = (x * lax.rsqrt(ms + EPS) * g_ref[...]).astype(BF16)
    h = h_scr[...]

    def norm_rope(z, gain, c, sa, sb):
        z = z * lax.rsqrt(jnp.mean(z * z, axis=-1, keepdims=True) + EPS) * gain
        fwd = pltpu.roll(z, HEAD_DIM - HEAD_DIM // 4, 1)
        bwd = pltpu.roll(z, HEAD_DIM // 4, 1)
        return z * c + fwd * sa + bwd * sb

    cos, sa, sb = cos_ref[...], sa_ref[...], sb_ref[...]
    qg = qg_ref[...] * (HEAD_DIM ** -0.5 * LOG2_E)
    kg = kg_ref[...]

    half = ATTN_WIDTH // 2
    for c in range(2):
        z = jnp.dot(h, w_ref[:, c * half:(c + 1) * half], preferred_element_type=F32)
        for i in range(half // HEAD_DIM):
            zi = z[:, i * HEAD_DIM:(i + 1) * HEAD_DIM]
            q_ref[c * (half // HEAD_DIM) + i] = norm_rope(zi, qg, cos, sa, sb).astype(BF16)

    z = jnp.dot(h, w_ref[:, ATTN_WIDTH:ATTN_WIDTH + 2 * KV_WIDTH], preferred_element_type=F32)
    for i in range(N_KV_HEADS):
        zi = z[:, i * HEAD_DIM:(i + 1) * HEAD_DIM]
        k_ref[i] = norm_rope(zi, kg, cos, sa, sb).T.astype(BF16)
        v_ref[i, :, 0:HEAD_DIM] = z[:, KV_WIDTH + i * HEAD_DIM:KV_WIDTH + (i + 1) * HEAD_DIM].astype(BF16)
        v_ref[i, :, HEAD_DIM:] = jnp.ones((x.shape[0], HEAD_DIM), BF16)

    u0 = ATTN_WIDTH + 2 * KV_WIDTH
    cw = CONV_WIDTH // 2
    for c in range(2):
        a = jnp.dot(h, w_ref[:, u0 + c * cw:u0 + (c + 1) * cw], preferred_element_type=F32)
        a = a + b_ref[:, c * cw:(c + 1) * cw]
        gt = jnp.dot(h, w_ref[:, u0 + CONV_WIDTH + c * cw:u0 + CONV_WIDTH + (c + 1) * cw],
                     preferred_element_type=F32)
        gt = gt + b_ref[:, CONV_WIDTH + c * cw:CONV_WIDTH + (c + 1) * cw]
        u_ref[:, c * cw:(c + 1) * cw] = (a * _sigmoid(gt)).astype(BF16)


def _in_projection(x2d, seq, norm_g, w_in, b_glu, q_g, k_g, tables, cast_weights=()):
    n = x2d.shape[0]
    tm = TM_PROJ
    steps = n // tm
    tiles_per_seq = seq // tm
    row = lambda i: (i, 0)
    tab = lambda i: (i % tiles_per_seq, 0)
    head_major = lambda i: (0, i, 0)
    tab_spec = pl.BlockSpec((tm, HEAD_DIM), tab)
    assert all(w.shape[0] % (steps * BF16_ROWS) == 0 for w in cast_weights)
    slab_specs = [pl.BlockSpec((w.shape[0] // steps, w.shape[1]), row) for w in cast_weights]
    return pl.pallas_call(
        functools.partial(_inproj_kernel, n_cast=len(cast_weights)),
        grid=(steps,),
        in_specs=[
            pl.BlockSpec((tm, D_MODEL), row),
            _const_spec((1, D_MODEL)),
            _const_spec((D_MODEL, IN_COLS)),
            _const_spec((1, 2 * CONV_WIDTH)),
            _const_spec((1, HEAD_DIM)),
            _const_spec((1, HEAD_DIM)),
            tab_spec, tab_spec, tab_spec,
            *slab_specs,
        ],
        out_specs=[
            pl.BlockSpec((N_Q_HEADS, tm, HEAD_DIM), head_major),
            pl.BlockSpec((N_KV_HEADS, HEAD_DIM, tm), lambda i: (0, 0, i)),
            pl.BlockSpec((N_KV_HEADS, tm, 2 * HEAD_DIM), head_major),
            pl.BlockSpec((tm, CONV_WIDTH), row),
            *slab_specs,
        ],
        out_shape=[
            jax.ShapeDtypeStruct((N_Q_HEADS, n, HEAD_DIM), BF16),
            jax.ShapeDtypeStruct((N_KV_HEADS, HEAD_DIM, n), BF16),
            jax.ShapeDtypeStruct((N_KV_HEADS, n, 2 * HEAD_DIM), BF16),
            jax.ShapeDtypeStruct((n, CONV_WIDTH), BF16),
            *[jax.ShapeDtypeStruct(w.shape, BF16) for w in cast_weights],
        ],
        scratch_shapes=[pltpu.VMEM((tm, D_MODEL), BF16)],
        compiler_params=pltpu.CompilerParams(
            dimension_semantics=("arbitrary",), vmem_limit_bytes=VMEM_LIMIT_BYTES),
        name="in_projection",
    )(x2d, norm_g, w_in, b_glu, q_g, k_g, *tables, *cast_weights)


def _attn_kernel(q_ref, k_ref, v_ref, o_ref, s0, s1, m_scr, acc_scr, *, tq, tk, nq, nk):
    m_rows = Q_PER_KV * tq
    n_pairs = nq * nk
    s_buf = (s0, s1)

    def split(t):
        t = jnp.minimum(t, n_pairs - 1)
        return t // nk, t % nk

    def stage_a(t, par):
        qi, kj = split(t)
        q = q_ref[:, pl.ds(pl.multiple_of(qi * tq, tq), tq), :].reshape(m_rows, HEAD_DIM)
        k_t = k_ref[0, :, pl.ds(pl.multiple_of(kj * tk, tk), tk)]
        s_buf[par][...] = jnp.dot(q, k_t, preferred_element_type=F32)

    def stage_b(t, par):
        qi, kj = split(t)
        v = v_ref[0, pl.ds(pl.multiple_of(kj * tk, tk), tk), :]
        rows = pl.ds(pl.multiple_of(qi * tq, tq), tq)
        for g in range(Q_PER_KV):
            head = slice(g * tq, (g + 1) * tq)
            tile_max = jnp.max(s_buf[par][head, :], axis=1, keepdims=True)
            m_prev = jnp.where(kj == 0, -jnp.inf, m_scr[head, :])
            m_new = jnp.maximum(m_prev, tile_max)
            alpha = jnp.exp2(m_prev - m_new)
            m_scr[head, :] = m_new
            p = jnp.exp2(s_buf[par][head, :] - m_new).astype(BF16)
            acc = acc_scr[head, :] * alpha
            acc = acc + jnp.dot(p, v, preferred_element_type=F32)
            acc_scr[head, :] = acc
            o = acc[:, 0:HEAD_DIM] / acc[:, HEAD_DIM:HEAD_DIM + 1]
            o_ref[rows, g * HEAD_DIM:(g + 1) * HEAD_DIM] = o.astype(BF16)

    m_scr[...] = jnp.zeros(m_scr.shape, F32)
    acc_scr[...] = jnp.zeros(acc_scr.shape, F32)
    stage_a(0, 0)

    def body(i, carry):
        for par in range(2):
            t = 2 * i + par
            stage_a(t + 1, 1 - par)
            stage_b(t, par)
        return carry

    lax.fori_loop(0, n_pairs // 2, body, 0)


def _attention(q, k, v, batch, seq):
    n = batch * seq
    tq = TQ_ATTN
    tk = min(TK_ATTN, seq)
    qs = min(QS_ATTN, seq)
    nq, nk, nsuper = qs // tq, seq // tk, seq // qs
    assert (nq * nk) % 2 == 0
    m = Q_PER_KV * tq
    kv_map = lambda b, h, si: (h, b, 0)
    return pl.pallas_call(
        functools.partial(_attn_kernel, tq=tq, tk=tk, nq=nq, nk=nk),
        grid=(batch, N_KV_HEADS, nsuper),
        in_specs=[
            pl.BlockSpec((Q_PER_KV, qs, HEAD_DIM), lambda b, h, si: (h, b * nsuper + si, 0)),
            pl.BlockSpec((1, HEAD_DIM, seq), lambda b, h, si: (h, 0, b)),
            pl.BlockSpec((1, seq, 2 * HEAD_DIM), kv_map),
        ],
        out_specs=pl.BlockSpec((qs, Q_PER_KV * HEAD_DIM), lambda b, h, si: (b * nsuper + si, h)),
        out_shape=jax.ShapeDtypeStruct((n, ATTN_WIDTH), BF16),
        scratch_shapes=[
            pltpu.VMEM((m, tk), F32),
            pltpu.VMEM((m, tk), F32),
            pltpu.VMEM((m, 1), F32),
            pltpu.VMEM((m, 2 * HEAD_DIM), F32),
        ],
        compiler_params=pltpu.CompilerParams(
            dimension_semantics=("arbitrary",) * 3, vmem_limit_bytes=VMEM_LIMIT_BYTES),
        name="attention",
    )(q, k, v)


def _conv_kernel(prev_ref, main_ref, next_ref, w_ref, b_ref, lg_ref, lb_ref, o_ref,
                 ext_scr, sh_scr, y_scr, *, ts, tiles_per_seq):
    si = pl.program_id(0) % tiles_per_seq
    halo = HALO_ROWS
    lanes = HEAD_DIM
    nblk = CONV_WIDTH // lanes
    span = ts + 2 * halo - SUBLANES
    base = halo - CONV_PAD
    rows = ROWS_CONV
    groups = rows // SUBLANES
    has_prev = si > 0
    has_next = si < tiles_per_seq - 1

    for cb in range(nblk):
        cs = slice(cb * lanes, (cb + 1) * lanes)
        ext_scr[cb, 0:halo, :] = jnp.where(has_prev, prev_ref[:, cs].astype(F32), 0.0)
        ext_scr[cb, halo:halo + ts, :] = main_ref[:, cs].astype(F32)
        ext_scr[cb, halo + ts:halo + ts + halo, :] = jnp.where(has_next, next_ref[:, cs].astype(F32), 0.0)
        for res in range(1, SUBLANES):
            sh_scr[cb, res - 1] = ext_scr[cb, res:res + span, :]
        taps = [jnp.broadcast_to(w_ref[j:j + 1, cs], (SUBLANES, lanes)) for j in range(CONV_TAPS)]
        bias = jnp.broadcast_to(b_ref[:, cs], (SUBLANES, lanes))

        def row_block(rb, carry, cb=cb, taps=taps, bias=bias):
            r0 = pl.multiple_of(rb * rows, rows)
            accs = [bias] * groups
            for j in range(CONV_TAPS):
                off = base + j
                res, start = off % SUBLANES, r0 + (off // SUBLANES) * SUBLANES
                if res == 0:
                    win = ext_scr[cb, pl.ds(start, rows), :]
                else:
                    win = sh_scr[cb, res - 1, pl.ds(start, rows), :]
                for r in range(groups):
                    accs[r] = accs[r] + win[r * SUBLANES:(r + 1) * SUBLANES] * taps[j]
            y_scr[cb, pl.ds(r0, rows), :] = jnp.concatenate(accs, axis=0)
            return carry

        lax.fori_loop(0, ts // rows, row_block, 0)

    def plane(cb):
        return y_scr[cb, 0:ts, :]

    tot = plane(0)
    for cb in range(1, nblk):
        tot = tot + plane(cb)
    mu = jnp.sum(tot, axis=-1, keepdims=True) * (1.0 / CONV_WIDTH)
    sq = None
    for cb in range(nblk):
        d = plane(cb) - mu
        sq = d * d if sq is None else sq + d * d
    var = jnp.sum(sq, axis=-1, keepdims=True) * (1.0 / CONV_WIDTH)
    inv = lax.rsqrt(var + EPS)
    for cb in range(nblk):
        cs = slice(cb * lanes, (cb + 1) * lanes)
        z = (plane(cb) - mu) * inv * lg_ref[:, cs] + lb_ref[:, cs]
        o_ref[:, cs] = (z * _sigmoid(z)).astype(BF16)


def _conv_module(u, seq, dw_w, dw_b, ln_g, ln_b):
    n = u.shape[0]
    ts = TS_CONV
    tiles_per_seq = seq // ts
    hb = ts // HALO_ROWS
    last = n // HALO_ROWS - 1
    span = ts + 2 * HALO_ROWS - SUBLANES
    nblk = CONV_WIDTH // HEAD_DIM
    return pl.pallas_call(
        functools.partial(_conv_kernel, ts=ts, tiles_per_seq=tiles_per_seq),
        grid=(n // ts,),
        in_specs=[
            pl.BlockSpec((HALO_ROWS, CONV_WIDTH), lambda i: (jnp.maximum(i * hb - 1, 0), 0)),
            pl.BlockSpec((ts, CONV_WIDTH), lambda i: (i, 0)),
            pl.BlockSpec((HALO_ROWS, CONV_WIDTH), lambda i: (jnp.minimum((i + 1) * hb, last), 0)),
            _const_spec((CONV_TAPS, CONV_WIDTH)),
            _const_spec((1, CONV_WIDTH)),
            _const_spec((1, CONV_WIDTH)),
            _const_spec((1, CONV_WIDTH)),
        ],
        out_specs=pl.BlockSpec((ts, CONV_WIDTH), lambda i: (i, 0)),
        out_shape=jax.ShapeDtypeStruct((n, CONV_WIDTH), BF16),
        scratch_shapes=[
            pltpu.VMEM((nblk, ts + 2 * HALO_ROWS, HEAD_DIM), F32),
            pltpu.VMEM((nblk, SUBLANES - 1, span, HEAD_DIM), F32),
            pltpu.VMEM((nblk, ts + SUBLANES, HEAD_DIM), F32),
        ],
        compiler_params=pltpu.CompilerParams(
            dimension_semantics=("arbitrary",), vmem_limit_bytes=VMEM_LIMIT_BYTES),
        name="conv_module",
    )(u, u, u, dw_w, dw_b, ln_g, ln_b)


def _ffn_kernel(x_ref, a_ref, c_ref, wo_ref, g_ref, wg_ref, wu_ref, wd_ref, o_ref, h_scr):
    f = pl.program_id(1)

    @pl.when(f == 0)
    def _():
        mix = jnp.dot(a_ref[...], wo_ref[0:ATTN_WIDTH, :], preferred_element_type=F32)
        mix = mix + jnp.dot(c_ref[...], wo_ref[ATTN_WIDTH:, :], preferred_element_type=F32)
        x1 = x_ref[...] + mix
        o_ref[...] = x1
        ms = jnp.mean(x1 * x1, axis=-1, keepdims=True)
        h_scr[...] = (x1 * lax.rsqrt(ms + EPS) * g_ref[...]).astype(BF16)

    h = h_scr[...]
    g = jnp.dot(h, wg_ref[...], preferred_element_type=F32)
    u = jnp.dot(h, wu_ref[...], preferred_element_type=F32)
    act = (g * _sigmoid(g) * u).astype(BF16)
    o_ref[...] += jnp.dot(act, wd_ref[...], preferred_element_type=F32)


def _out_ffn(x2d, attn, conv, w_out, norm_g, w_gate, w_up, w_down):
    n = x2d.shape[0]
    tm, tf = TM_FFN, TF_FFN
    return pl.pallas_call(
        _ffn_kernel,
        grid=(n // tm, D_FF // tf),
        in_specs=[
            pl.BlockSpec((tm, D_MODEL), lambda i, f: (i, 0)),
            pl.BlockSpec((tm, ATTN_WIDTH), lambda i, f: (i, 0)),
            pl.BlockSpec((tm, CONV_WIDTH), lambda i, f: (i, 0)),
            _const_spec((D_MODEL, D_MODEL)),
            _const_spec((1, D_MODEL)),
            pl.BlockSpec((D_MODEL, tf), lambda i, f: (0, f)),
            pl.BlockSpec((D_MODEL, tf), lambda i, f: (0, f)),
            pl.BlockSpec((tf, D_MODEL), lambda i, f: (f, 0)),
        ],
        out_specs=pl.BlockSpec((tm, D_MODEL), lambda i, f: (i, 0)),
        out_shape=jax.ShapeDtypeStruct((n, D_MODEL), F32),
        scratch_shapes=[pltpu.VMEM((tm, D_MODEL), BF16)],
        compiler_params=pltpu.CompilerParams(
            dimension_semantics=("arbitrary", "arbitrary"), vmem_limit_bytes=VMEM_LIMIT_BYTES),
        name="out_ffn",
    )(x2d, attn, conv, w_out, norm_g, w_gate, w_up, w_down)


def _rope_tables(seq):
    rows = seq // GRID_W
    half = HEAD_DIM // 2
    inv_freq = ROPE_THETA ** (-jnp.arange(0, half, 2, dtype=F32) / half)
    ang_r = jnp.arange(rows).astype(F32)[:, None] * inv_freq[None, :]
    ang_c = jnp.arange(GRID_W).astype(F32)[:, None] * inv_freq[None, :]
    small = lax.optimization_barrier((jnp.cos(ang_r), jnp.sin(ang_r), jnp.cos(ang_c), jnp.sin(ang_c)))

    def expand(by_row, by_col):
        r = jnp.repeat(by_row, GRID_W, axis=0)
        c = jnp.tile(by_col, (rows, 1))
        return jnp.concatenate([r, r, c, c], axis=-1)

    cos, sin = expand(small[0], small[2]), expand(small[1], small[3])
    first = (jnp.arange(HEAD_DIM) // (HEAD_DIM // 4)) % 2 == 0
    sin_a = jnp.where(first[None, :], -sin, 0.0)
    sin_b = jnp.where(first[None, :], 0.0, sin)
    return (cos, sin_a, sin_b)


def _trunk(x, p):
    batch, seq, d_model = x.shape
    assert d_model == D_MODEL and x.dtype == F32
    assert seq % GRID_W == 0 and seq % TM_PROJ == 0 and seq % TS_CONV == 0
    assert seq % min(TK_ATTN, seq) == 0 and seq % min(QS_ATTN, seq) == 0 and min(QS_ATTN, seq) % TQ_ATTN == 0
    assert (batch * seq) % TM_FFN == 0 and TS_CONV % ROWS_CONV == 0 and D_FF % TF_FFN == 0
    x2d = x.reshape(batch * seq, D_MODEL)
    todo = () if "ffn_bf16" in p else p["ffn_f32"]
    q, k, v, u, *cast = _in_projection(x2d, seq, p["norm_mix_g"], p["w_in"], p["b_glu"],
                                       p["q_norm_g"], p["k_norm_g"], _rope_tables(seq), todo)
    if todo:
        p = dict(p, ffn_bf16=tuple(cast))
    attn = _attention(q, k, v, batch, seq)
    conv = _conv_module(u, seq, p["dw_w"], p["dw_b"], p["conv_ln_g"], p["conv_ln_b"])
    y = _out_ffn(x2d, attn, conv, p["w_out"], p["norm_ffn_g"], *p["ffn_bf16"])
    return y.reshape(batch, seq, D_MODEL), p


def _prepare(norm_mix_g, w_in, b_glu, q_norm_g, k_norm_g, dw_w, dw_b,
             conv_ln_g, conv_ln_b, w_out, norm_ffn_g, w_gate, w_up, w_down):
    assert norm_mix_g.shape[0] == 1, "single-layer trunk"
    return {
        "norm_mix_g": norm_mix_g[0][None, :],
        "w_in": w_in[0].astype(BF16),
        "b_glu": b_glu[0][None, :],
        "q_norm_g": q_norm_g[0][None, :],
        "k_norm_g": k_norm_g[0][None, :],
        "dw_w": dw_w[0],
        "dw_b": dw_b[0][None, :],
        "conv_ln_g": conv_ln_g[0][None, :],
        "conv_ln_b": conv_ln_b[0][None, :],
        "w_out": w_out[0].astype(BF16),
        "norm_ffn_g": norm_ffn_g[0][None, :],
        "ffn_f32": (w_gate[0], w_up[0], w_down[0]),
    }


def kernel(x_prompt, x_sample, norm_mix_g, w_in, b_glu, q_norm_g, k_norm_g, dw_w, dw_b,
           conv_ln_g, conv_ln_b, w_out, norm_ffn_g, w_gate, w_up, w_down):
    p = _prepare(norm_mix_g, w_in, b_glu, q_norm_g, k_norm_g, dw_w, dw_b,
                 conv_ln_g, conv_ln_b, w_out, norm_ffn_g, w_gate, w_up, w_down)
    y_prompt, p = _trunk(x_prompt, p)
    y_sample, _ = _trunk(x_sample, p)
    return (y_prompt, y_sample)
```

```python
import functools

import jax
import jax.numpy as jnp
from jax import lax
from jax.experimental import pallas as pl
from jax.experimental.pallas import tpu as pltpu

D_MODEL = 2048
GRID_W = 64
HEAD_DIM = 128
N_Q_HEADS = 8
N_KV_HEADS = 2
Q_PER_KV = N_Q_HEADS // N_KV_HEADS
ATTN_WIDTH = N_Q_HEADS * HEAD_DIM
KV_WIDTH = N_KV_HEADS * HEAD_DIM
CONV_WIDTH = D_MODEL - ATTN_WIDTH
CONV_TAPS = 31
CONV_PAD = CONV_TAPS // 2
IN_COLS = ATTN_WIDTH + 2 * KV_WIDTH + 2 * CONV_WIDTH
D_FF = 5632
ROPE_THETA = 10000.0
EPS = 1e-6
LOG2_E = 1.4426950408889634

F32 = jnp.float32
BF16 = jnp.bfloat16

V7X_VMEM_BYTES = 64 * 1024 * 1024
VMEM_LIMIT_BYTES = V7X_VMEM_BYTES - 8 * 1024 * 1024
SUBLANES = 8
BF16_ROWS = 16
HALO_ROWS = BF16_ROWS
assert HALO_ROWS >= CONV_PAD

TM_PROJ = 512
TQ_ATTN = 256
TK_ATTN = 2048
QS_ATTN = 4096
TS_CONV = 512
ROWS_CONV = 128
TM_FFN = 512
TF_FFN = 512


def _const_spec(shape):
    return pl.BlockSpec(shape, lambda *_: (0,) * len(shape), pipeline_mode=pl.Buffered(1))


def _sigmoid(x):
    return 1.0 / (1.0 + jnp.exp(-x))


def _inproj_kernel(x_ref, g_ref, w_ref, b_ref, qg_ref, kg_ref, cos_ref, sa_ref, sb_ref, *rest, n_cast):
    cast_in, (q_ref, k_ref, v_ref, u_ref), cast_out, h_scr = (
        rest[:n_cast], rest[n_cast:n_cast + 4], rest[n_cast + 4:2 * n_cast + 4], rest[2 * n_cast + 4])
    for src, dst in zip(cast_in, cast_out):
        dst[...] = src[...].astype(BF16)
    x = x_ref[...]
    ms = jnp.mean(x * x, axis=-1, keepdims=True)
    h_scr[...] = (x * lax.rsqrt(ms + EPS) * g_ref[...]).astype(BF16)
    h = h_scr[...]

    def norm_rope(z, gain, c, sa, sb):
        z = z * lax.rsqrt(jnp.mean(z * z, axis=-1, keepdims=True) + EPS) * gain
        fwd = pltpu.roll(z, HEAD_DIM - HEAD_DIM // 4, 1)
        bwd = pltpu.roll(z, HEAD_DIM // 4, 1)
        return z * c + fwd * sa + bwd * sb

    cos, sa, sb = cos_ref[...], sa_ref[...], sb_ref[...]
    qg = qg_ref[...] * (HEAD_DIM ** -0.5 * LOG2_E)
    kg = kg_ref[...]

    half = ATTN_WIDTH // 2
    for c in range(2):
        z = jnp.dot(h, w_ref[:, c * half:(c + 1) * half], preferred_element_type=F32)
        for i in range(half // HEAD_DIM):
            zi = z[:, i * HEAD_DIM:(i + 1) * HEAD_DIM]
            q_ref[c * (half // HEAD_DIM) + i] = norm_rope(zi, qg, cos, sa, sb).astype(BF16)

    z = jnp.dot(h, w_ref[:, ATTN_WIDTH:ATTN_WIDTH + 2 * KV_WIDTH], preferred_element_type=F32)
    for i in range(N_KV_HEADS):
        zi = z[:, i * HEAD_DIM:(i + 1) * HEAD_DIM]
        k_ref[i] = norm_rope(zi, kg, cos, sa, sb).T.astype(BF16)
        v_ref[i, :, 0:HEAD_DIM] = z[:, KV_WIDTH + i * HEAD_DIM:KV_WIDTH + (i + 1) * HEAD_DIM].astype(BF16)
        v_ref[i, :, HEAD_DIM:] = jnp.ones((x.shape[0], HEAD_DIM), BF16)

    u0 = ATTN_WIDTH + 2 * KV_WIDTH
    cw = CONV_WIDTH // 2
    for c in range(2):
        a = jnp.dot(h, w_ref[:, u0 + c * cw:u0 + (c + 1) * cw], preferred_element_type=F32)
        a = a + b_ref[:, c * cw:(c + 1) * cw]
        gt = jnp.dot(h, w_ref[:, u0 + CONV_WIDTH + c * cw:u0 + CONV_WIDTH + (c + 1) * cw],
                     preferred_element_type=F32)
        gt = gt + b_ref[:, CONV_WIDTH + c * cw:CONV_WIDTH + (c + 1) * cw]
        u_ref[:, c * cw:(c + 1) * cw] = (a * _sigmoid(gt)).astype(BF16)


def _in_projection(x2d, seq, norm_g, w_in, b_glu, q_g, k_g, tables, cast_weights=()):
    n = x2d.shape[0]
    tm = TM_PROJ
    steps = n // tm
    tiles_per_seq = seq // tm
    row = lambda i: (i, 0)
    tab = lambda i: (i % tiles_per_seq, 0)
    head_major = lambda i: (0, i, 0)
    tab_spec = pl.BlockSpec((tm, HEAD_DIM), tab)
    assert all(w.shape[0] % (steps * BF16_ROWS) == 0 for w in cast_weights)
    slab_specs = [pl.BlockSpec((w.shape[0] // steps, w.shape[1]), row) for w in cast_weights]
    return pl.pallas_call(
        functools.partial(_inproj_kernel, n_cast=len(cast_weights)),
        grid=(steps,),
        in_specs=[
            pl.BlockSpec((tm, D_MODEL), row),
            _const_spec((1, D_MODEL)),
            _const_spec((D_MODEL, IN_COLS)),
            _const_spec((1, 2 * CONV_WIDTH)),
            _const_spec((1, HEAD_DIM)),
            _const_spec((1, HEAD_DIM)),
            tab_spec, tab_spec, tab_spec,
            *slab_specs,
        ],
        out_specs=[
            pl.BlockSpec((N_Q_HEADS, tm, HEAD_DIM), head_major),
            pl.BlockSpec((N_KV_HEADS, HEAD_DIM, tm), lambda i: (0, 0, i)),
            pl.BlockSpec((N_KV_HEADS, tm, 2 * HEAD_DIM), head_major),
            pl.BlockSpec((tm, CONV_WIDTH), row),
            *slab_specs,
        ],
        out_shape=[
            jax.ShapeDtypeStruct((N_Q_HEADS, n, HEAD_DIM), BF16),
            jax.ShapeDtypeStruct((N_KV_HEADS, HEAD_DIM, n), BF16),
            jax.ShapeDtypeStruct((N_KV_HEADS, n, 2 * HEAD_DIM), BF16),
            jax.ShapeDtypeStruct((n, CONV_WIDTH), BF16),
            *[jax.ShapeDtypeStruct(w.shape, BF16) for w in cast_weights],
        ],
        scratch_shapes=[pltpu.VMEM((tm, D_MODEL), BF16)],
        compiler_params=pltpu.CompilerParams(
            dimension_semantics=("arbitrary",), vmem_limit_bytes=VMEM_LIMIT_BYTES),
        name="in_projection",
    )(x2d, norm_g, w_in, b_glu, q_g, k_g, *tables, *cast_weights)


def _attn_kernel(q_ref, k_ref, v_ref, o_ref, s0, s1, m_scr, acc_scr, *, tq, tk, nq, nk):
    m_rows = Q_PER_KV * tq
    n_pairs = nq * nk
    s_buf = (s0, s1)

    def split(t):
        t = jnp.minimum(t, n_pairs - 1)
        return t // nk, t % nk

    def stage_a(t, par):
        qi, kj = split(t)
        q = q_ref[:, pl.ds(pl.multiple_of(qi * tq, tq), tq), :].reshape(m_rows, HEAD_DIM)
        k_t = k_ref[0, :, pl.ds(pl.multiple_of(kj * tk, tk), tk)]
        s_buf[par][...] = jnp.dot(q, k_t, preferred_element_type=F32)

    def stage_b(t, par):
        qi, kj = split(t)
        v = v_ref[0, pl.ds(pl.multiple_of(kj * tk, tk), tk), :]
        rows = pl.ds(pl.multiple_of(qi * tq, tq), tq)
        for g in range(Q_PER_KV):
            head = slice(g * tq, (g + 1) * tq)
            tile_max = jnp.max(s_buf[par][head, :], axis=1, keepdims=True)
            m_prev = jnp.where(kj == 0, -jnp.inf, m_scr[head, :])
            m_new = jnp.maximum(m_prev, tile_max)
            alpha = jnp.exp2(m_prev - m_new)
            m_scr[head, :] = m_new
            p = jnp.exp2(s_buf[par][head, :] - m_new).astype(BF16)
            acc = acc_scr[head, :] * alpha
            acc = acc + jnp.dot(p, v, preferred_element_type=F32)
            acc_scr[head, :] = acc
            o = acc[:, 0:HEAD_DIM] / acc[:, HEAD_DIM:HEAD_DIM + 1]
            o_ref[rows, g * HEAD_DIM:(g + 1) * HEAD_DIM] = o.astype(BF16)

    m_scr[...] = jnp.zeros(m_scr.shape, F32)
    acc_scr[...] = jnp.zeros(acc_scr.shape, F32)
    stage_a(0, 0)

    def body(i, carry):
        for par in range(2):
            t = 2 * i + par
            stage_a(t + 1, 1 - par)
            stage_b(t, par)
        return carry

    lax.fori_loop(0, n_pairs // 2, body, 0)


def _attention(q, k, v, batch, seq):
    n = batch * seq
    tq = TQ_ATTN
    tk = min(TK_ATTN, seq)
    qs = min(QS_ATTN, seq)
    nq, nk, nsuper = qs // tq, seq // tk, seq // qs
    assert (nq * nk) % 2 == 0
    m = Q_PER_KV * tq
    kv_map = lambda b, h, si: (h, b, 0)
    return pl.pallas_call(
        functools.partial(_attn_kernel, tq=tq, tk=tk, nq=nq, nk=nk),
        grid=(batch, N_KV_HEADS, nsuper),
        in_specs=[
            pl.BlockSpec((Q_PER_KV, qs, HEAD_DIM), lambda b, h, si: (h, b * nsuper + si, 0)),
            pl.BlockSpec((1, HEAD_DIM, seq), lambda b, h, si: (h, 0, b)),
            pl.BlockSpec((1, seq, 2 * HEAD_DIM), kv_map),
        ],
        out_specs=pl.BlockSpec((qs, Q_PER_KV * HEAD_DIM), lambda b, h, si: (b * nsuper + si, h)),
        out_shape=jax.ShapeDtypeStruct((n, ATTN_WIDTH), BF16),
        scratch_shapes=[
            pltpu.VMEM((m, tk), F32),
            pltpu.VMEM((m, tk), F32),
            pltpu.VMEM((m, 1), F32),
            pltpu.VMEM((m, 2 * HEAD_DIM), F32),
        ],
        compiler_params=pltpu.CompilerParams(
            dimension_semantics=("arbitrary",) * 3, vmem_limit_bytes=VMEM_LIMIT_BYTES),
        name="attention",
    )(q, k, v)


def _conv_kernel(prev_ref, main_ref, next_ref, w_ref, b_ref, lg_ref, lb_ref, o_ref,
                 ext_scr, sh_scr, y_scr, *, ts, tiles_per_seq):
    si = pl.program_id(0) % tiles_per_seq
    halo = HALO_ROWS
    lanes = HEAD_DIM
    nblk = CONV_WIDTH // lanes
    span = ts + 2 * halo - SUBLANES
    base = halo - CONV_PAD
    rows = ROWS_CONV
    groups = rows // SUBLANES
    has_prev = si > 0
    has_next = si < tiles_per_seq - 1

    for cb in range(nblk):
        cs = slice(cb * lanes, (cb + 1) * lanes)
        ext_scr[cb, 0:halo, :] = jnp.where(has_prev, prev_ref[:, cs].astype(F32), 0.0)
        ext_scr[cb, halo:halo + ts, :] = main_ref[:, cs].astype(F32)
        ext_scr[cb, halo + ts:halo + ts + halo, :] = jnp.where(has_next, next_ref[:, cs].astype(F32), 0.0)
        for res in range(1, SUBLANES):
            sh_scr[cb, res - 1] = ext_scr[cb, res:res + span, :]
        taps = [jnp.broadcast_to(w_ref[j:j + 1, cs], (SUBLANES, lanes)) for j in range(CONV_TAPS)]
        bias = jnp.broadcast_to(b_ref[:, cs], (SUBLANES, lanes))

        def row_block(rb, carry, cb=cb, taps=taps, bias=bias):
            r0 = pl.multiple_of(rb * rows, rows)
            accs = [bias] * groups
            for j in range(CONV_TAPS):
                off = base + j
                res, start = off % SUBLANES, r0 + (off // SUBLANES) * SUBLANES
                if res == 0:
                    win = ext_scr[cb, pl.ds(start, rows), :]
                else:
                    win = sh_scr[cb, res - 1, pl.ds(start, rows), :]
                for r in range(groups):
                    accs[r] = accs[r] + win[r * SUBLANES:(r + 1) * SUBLANES] * taps[j]
            y_scr[cb, pl.ds(r0, rows), :] = jnp.concatenate(accs, axis=0)
            return carry

        lax.fori_loop(0, ts // rows, row_block, 0)

    def plane(cb):
        return y_scr[cb, 0:ts, :]

    tot = plane(0)
    for cb in range(1, nblk):
        tot = tot + plane(cb)
    mu = jnp.sum(tot, axis=-1, keepdims=True) * (1.0 / CONV_WIDTH)
    sq = None
    for cb in range(nblk):
        d = plane(cb) - mu
        sq = d * d if sq is None else sq + d * d
    var = jnp.sum(sq, axis=-1, keepdims=True) * (1.0 / CONV_WIDTH)
    inv = lax.rsqrt(var + EPS)
    for cb in range(nblk):
        cs = slice(cb * lanes, (cb + 1) * lanes)
        z = (plane(cb) - mu) * inv * lg_ref[:, cs] + lb_ref[:, cs]
        o_ref[:, cs] = (z * _sigmoid(z)).astype(BF16)


def _conv_module(u, seq, dw_w, dw_b, ln_g, ln_b):
    n = u.shape[0]
    ts = TS_CONV
    tiles_per_seq = seq // ts
    hb = ts // HALO_ROWS
    last = n // HALO_ROWS - 1
    span = ts + 2 * HALO_ROWS - SUBLANES
    nblk = CONV_WIDTH // HEAD_DIM
    return pl.pallas_call(
        functools.partial(_conv_kernel, ts=ts, tiles_per_seq=tiles_per_seq),
        grid=(n // ts,),
        in_specs=[
            pl.BlockSpec((HALO_ROWS, CONV_WIDTH), lambda i: (jnp.maximum(i * hb - 1, 0), 0)),
            pl.BlockSpec((ts, CONV_WIDTH), lambda i: (i, 0)),
            pl.BlockSpec((HALO_ROWS, CONV_WIDTH), lambda i: (jnp.minimum((i + 1) * hb, last), 0)),
            _const_spec((CONV_TAPS, CONV_WIDTH)),
            _const_spec((1, CONV_WIDTH)),
            _const_spec((1, CONV_WIDTH)),
            _const_spec((1, CONV_WIDTH)),
        ],
        out_specs=pl.BlockSpec((ts, CONV_WIDTH), lambda i: (i, 0)),
        out_shape=jax.ShapeDtypeStruct((n, CONV_WIDTH), BF16),
        scratch_shapes=[
            pltpu.VMEM((nblk, ts + 2 * HALO_ROWS, HEAD_DIM), F32),
            pltpu.VMEM((nblk, SUBLANES - 1, span, HEAD_DIM), F32),
            pltpu.VMEM((nblk, ts + SUBLANES, HEAD_DIM), F32),
        ],
        compiler_params=pltpu.CompilerParams(
            dimension_semantics=("arbitrary",), vmem_limit_bytes=VMEM_LIMIT_BYTES),
        name="conv_module",
    )(u, u, u, dw_w, dw_b, ln_g, ln_b)


def _ffn_kernel(x_ref, a_ref, c_ref, wo_ref, g_ref, wg_ref, wu_ref, wd_ref, o_ref, h_scr):
    f = pl.program_id(1)

    @pl.when(f == 0)
    def _():
        mix = jnp.dot(a_ref[...], wo_ref[0:ATTN_WIDTH, :], preferred_element_type=F32)
        mix = mix + jnp.dot(c_ref[...], wo_ref[ATTN_WIDTH:, :], preferred_element_type=F32)
        x1 = x_ref[...] + mix
        o_ref[...] = x1
        ms = jnp.mean(x1 * x1, axis=-1, keepdims=True)
        h_scr[...] = (x1 * lax.rsqrt(ms + EPS) * g_ref[...]).astype(BF16)

    h = h_scr[...]
    g = jnp.dot(h, wg_ref[...], preferred_element_type=F32)
    u = jnp.dot(h, wu_ref[...], preferred_element_type=F32)
    act = (g * _sigmoid(g) * u).astype(BF16)
    o_ref[...] += jnp.dot(act, wd_ref[...], preferred_element_type=F32)


def _out_ffn(x2d, attn, conv, w_out, norm_g, w_gate, w_up, w_down):
    n = x2d.shape[0]
    tm, tf = TM_FFN, TF_FFN
    return pl.pallas_call(
        _ffn_kernel,
        grid=(n // tm, D_FF // tf),
        in_specs=[
            pl.BlockSpec((tm, D_MODEL), lambda i, f: (i, 0)),
            pl.BlockSpec((tm, ATTN_WIDTH), lambda i, f: (i, 0)),
            pl.BlockSpec((tm, CONV_WIDTH), lambda i, f: (i, 0)),
            _const_spec((D_MODEL, D_MODEL)),
            _const_spec((1, D_MODEL)),
            pl.BlockSpec((D_MODEL, tf), lambda i, f: (0, f)),
            pl.BlockSpec((D_MODEL, tf), lambda i, f: (0, f)),
            pl.BlockSpec((tf, D_MODEL), lambda i, f: (f, 0)),
        ],
        out_specs=pl.BlockSpec((tm, D_MODEL), lambda i, f: (i, 0)),
        out_shape=jax.ShapeDtypeStruct((n, D_MODEL), F32),
        scratch_shapes=[pltpu.VMEM((tm, D_MODEL), BF16)],
        compiler_params=pltpu.CompilerParams(
            dimension_semantics=("arbitrary", "arbitrary"), vmem_limit_bytes=VMEM_LIMIT_BYTES),
        name="out_ffn",
    )(x2d, attn, conv, w_out, norm_g, w_gate, w_up, w_down)


def _rope_tables(seq):
    rows = seq // GRID_W
    half = HEAD_DIM // 2
    inv_freq = ROPE_THETA ** (-jnp.arange(0, half, 2, dtype=F32) / half)
    ang_r = jnp.arange(rows).astype(F32)[:, None] * inv_freq[None, :]
    ang_c = jnp.arange(GRID_W).astype(F32)[:, None] * inv_freq[None, :]
    small = lax.optimization_barrier((jnp.cos(ang_r), jnp.sin(ang_r), jnp.cos(ang_c), jnp.sin(ang_c)))

    def expand(by_row, by_col):
        r = jnp.repeat(by_row, GRID_W, axis=0)
        c = jnp.tile(by_col, (rows, 1))
        return jnp.concatenate([r, r, c, c], axis=-1)

    cos, sin = expand(small[0], small[2]), expand(small[1], small[3])
    first = (jnp.arange(HEAD_DIM) // (HEAD_DIM // 4)) % 2 == 0
    sin_a = jnp.where(first[None, :], -sin, 0.0)
    sin_b = jnp.where(first[None, :], 0.0, sin)
    return (cos, sin_a, sin_b)


def _trunk(x, p):
    batch, seq, d_model = x.shape
    assert d_model == D_MODEL and x.dtype == F32
    assert seq % GRID_W == 0 and seq % TM_PROJ == 0 and seq % TS_CONV == 0
    assert seq % min(TK_ATTN, seq) == 0 and seq % min(QS_ATTN, seq) == 0 and min(QS_ATTN, seq) % TQ_ATTN == 0
    assert (batch * seq) % TM_FFN == 0 and TS_CONV % ROWS_CONV == 0 and D_FF % TF_FFN == 0
    x2d = x.reshape(batch * seq, D_MODEL)
    todo = () if "ffn_bf16" in p else p["ffn_f32"]
    q, k, v, u, *cast = _in_projection(x2d, seq, p["norm_mix_g"], p["w_in"], p["b_glu"],
                                       p["q_norm_g"], p["k_norm_g"], _rope_tables(seq), todo)
    if todo:
        p = dict(p, ffn_bf16=tuple(cast))
    attn = _attention(q, k, v, batch, seq)
    conv = _conv_module(u, seq, p["dw_w"], p["dw_b"], p["conv_ln_g"], p["conv_ln_b"])
    w_out, w_gate, w_up, w_down = p["ffn_bf16"]
    y = _out_ffn(x2d, attn, conv, w_out, p["norm_ffn_g"], w_gate, w_up, w_down)
    return y.reshape(batch, seq, D_MODEL), p


def _prepare(norm_mix_g, w_in, b_glu, q_norm_g, k_norm_g, dw_w, dw_b,
             conv_ln_g, conv_ln_b, w_out, norm_ffn_g, w_gate, w_up, w_down):
    assert norm_mix_g.shape[0] == 1, "single-layer trunk"
    return {
        "norm_mix_g": norm_mix_g[0][None, :],
        "w_in": w_in[0].astype(BF16),
        "b_glu": b_glu[0][None, :],
        "q_norm_g": q_norm_g[0][None, :],
        "k_norm_g": k_norm_g[0][None, :],
        "dw_w": dw_w[0],
        "dw_b": dw_b[0][None, :],
        "conv_ln_g": conv_ln_g[0][None, :],
        "conv_ln_b": conv_ln_b[0][None, :],
        "norm_ffn_g": norm_ffn_g[0][None, :],
        "ffn_f32": (w_out[0], w_gate[0], w_up[0], w_down[0]),
    }


def kernel(x_prompt, x_sample, norm_mix_g, w_in, b_glu, q_norm_g, k_norm_g, dw_w, dw_b,
           conv_ln_g, conv_ln_b, w_out, norm_ffn_g, w_gate, w_up, w_down):
    p = _prepare(norm_mix_g, w_in, b_glu, q_norm_g, k_norm_g, dw_w, dw_b,
                 conv_ln_g, conv_ln_b, w_out, norm_ffn_g, w_gate, w_up, w_down)
    y_prompt, p = _trunk(x_prompt, p)
    y_sample, _ = _trunk(x_sample, p)
    return (y_prompt, y_sample)
```
